```python
import math
import jax, jax.numpy as jnp
from jax import lax
import numpy as np

D_MODEL = 1024
BATCH = 16
SEQ = 2048
DEPTH = 1
DEC_BATCH = 1
DEC_SEQ = 16384
PAST_LEN = 128

PLE_DIM = 256
DA_HEADS = 4
DA_QK_DIM = 64
DA_V_DIM = 2 * DA_QK_DIM
DA_WIDTH = DA_HEADS * DA_V_DIM
GLA_HEADS = 4
GLA_DK = 64
GLA_DV = 128
GLA_WIDTH = GLA_HEADS * GLA_DV
GLA_GATE_RANK = 16
GLA_GATE_NORM = 16.0
GLA_CHUNK = 64
MIX_WIDTH = DA_WIDTH + GLA_WIDTH
D_FF = 4 * D_MODEL
CONV_WIDTH = 3
Q_BLOCK = 128
NORM_EPS = 1e-6

IN_SPLITS = (DA_HEADS * 2 * DA_QK_DIM,
             DA_HEADS * 2 * DA_QK_DIM,
             DA_WIDTH,
             GLA_HEADS * GLA_DK,
             GLA_HEADS * GLA_DK,
             GLA_WIDTH,
             GLA_WIDTH,
             GLA_GATE_RANK,
             GLA_GATE_RANK)
IN_WIDTH = sum(IN_SPLITS)

kernel_name = 'hymba_diffattn_gla_convffn_encoder'


def _split_points():
    pts, acc = [], 0
    for w in IN_SPLITS[:-1]:
        acc += w
        pts.append(acc)
    return pts


def rmsnorm(x, g):
    xf = x.astype(jnp.float32)
    y = xf * lax.rsqrt(jnp.mean(xf * xf, axis=-1, keepdims=True) + NORM_EPS)
    return (y * g.astype(jnp.float32)).astype(x.dtype)


def alibi_slopes(n):
    return jnp.asarray([2.0 ** (-8.0 * (h + 1) / n) for h in range(n)], jnp.float32)


def diff_attention(q, k, v, lam):
    B, S = q.shape[0], q.shape[1]
    nblk = S // Q_BLOCK
    scale = DA_QK_DIM ** -0.5
    slopes = alibi_slopes(DA_HEADS)
    kpos = jnp.arange(S, dtype=jnp.float32)
    qb = q.reshape(B, nblk, Q_BLOCK, DA_HEADS, 2, DA_QK_DIM).transpose(1, 0, 2, 3, 4, 5)

    def block(args):
        qi, blk = args
        qpos = (blk * Q_BLOCK + jnp.arange(Q_BLOCK)).astype(jnp.float32)
        dist = jnp.abs(qpos[:, None] - kpos[None, :])
        bias = -slopes[:, None, None] * dist[None]
        s = jnp.einsum('bqhmd,bkhmd->bhmqk', qi, k).astype(jnp.float32) * scale
        a = jax.nn.softmax(s + bias[None, :, None], axis=-1)
        w = a[:, :, 0] - lam * a[:, :, 1]
        return jnp.einsum('bhqk,bkhe->bqhe', w.astype(v.dtype), v)

    out = lax.map(block, (qb, jnp.arange(nblk)))
    return out.transpose(1, 0, 2, 3, 4).reshape(B, S, DA_HEADS, DA_V_DIM)


def gla_direction(q, k, v, g, include_diag):
    B, S, H, dk = q.shape
    dv = v.shape[-1]
    C = GLA_CHUNK
    N = S // C
    f32 = jnp.float32
    qc = q.astype(f32).reshape(B, N, C, H, dk)
    kc = k.astype(f32).reshape(B, N, C, H, dk)
    vc = v.astype(f32).reshape(B, N, C, H, dv)
    gc = g.astype(f32).reshape(B, N, C, H, dk)
    b = jnp.cumsum(gc, axis=2)
    b_last = b[:, :, -1]
    q_t = qc * jnp.exp(b)
    k_t = kc * jnp.exp(-b)
    k_end = kc * jnp.exp(b_last[:, :, None] - b)
    mask = jnp.tril(jnp.ones((C, C), dtype=bool), 0 if include_diag else -1)
    att = jnp.where(mask, jnp.einsum('bnchd,bnshd->bnhcs', q_t, k_t), 0.0)
    o_intra = jnp.einsum('bnhcs,bnshe->bnche', att, vc)
    kv = jnp.einsum('bnshd,bnshe->nbhde', k_end, vc)
    decay = jnp.exp(b_last).transpose(1, 0, 2, 3)

    def step(state, inp):
        kv_n, dec_n = inp
        return dec_n[..., None] * state + kv_n, state

    s0 = jnp.zeros((B, H, dk, dv), f32)
    _, s_prev = lax.scan(step, s0, (kv, decay))
    o_inter = jnp.einsum('bnchd,nbhde->bnche', q_t, s_prev)
    return (o_intra + o_inter).reshape(B, S, H, dv)


def dwconv3(u, w, b):
    up = jnp.pad(u, ((0, 0), (1, 1), (0, 0)))
    return up[:, :-2] * w[0] + up[:, 1:-1] * w[1] + up[:, 2:] * w[2] + b


def encoder_layer(x, p, lam_init, pre_mix_g, w_in, da_lq1, da_lk1, da_lq2, da_lk2,
                  da_norm_g, gla_w_gate_f, gla_b_gate_f, gla_w_gate_b, gla_b_gate_b,
                  gla_norm_g, w_out, post_mix_g, pre_ffn_g, w_ffn_up, ffn_conv_w,
                  ffn_conv_b, w_ffn_down, post_ffn_g, w_ple_gate, b_ple_gate,
                  w_ple_proj, ple_norm_g):
    B, S, _ = x.shape
    h = rmsnorm(x, pre_mix_g)
    proj = h @ w_in
    (da_q, da_k, da_v, gq, gk, gv, gr, lr_f, lr_b) = jnp.split(proj, _split_points(), axis=-1)

    lam = (jnp.exp(jnp.sum(da_lq1.astype(jnp.float32) * da_lk1.astype(jnp.float32)))
           - jnp.exp(jnp.sum(da_lq2.astype(jnp.float32) * da_lk2.astype(jnp.float32)))
           + lam_init)
    da = diff_attention(da_q.reshape(B, S, DA_HEADS, 2, DA_QK_DIM),
                        da_k.reshape(B, S, DA_HEADS, 2, DA_QK_DIM),
                        da_v.reshape(B, S, DA_HEADS, DA_V_DIM), lam)
    da = rmsnorm(da, da_norm_g) * (1.0 - lam_init)

    q = (gq * GLA_DK ** -0.5).reshape(B, S, GLA_HEADS, GLA_DK)
    k = gk.reshape(B, S, GLA_HEADS, GLA_DK)
    v = gv.reshape(B, S, GLA_HEADS, GLA_DV)
    g_f = (jax.nn.log_sigmoid((lr_f @ gla_w_gate_f + gla_b_gate_f).astype(jnp.float32))
           / GLA_GATE_NORM).reshape(B, S, GLA_HEADS, GLA_DK)
    g_b = (jax.nn.log_sigmoid((lr_b @ gla_w_gate_b + gla_b_gate_b).astype(jnp.float32))
           / GLA_GATE_NORM).reshape(B, S, GLA_HEADS, GLA_DK)
    o_f = gla_direction(q, k, v, g_f, True)
    o_b = jnp.flip(gla_direction(jnp.flip(q, 1), jnp.flip(k, 1), jnp.flip(v, 1),
                                 jnp.flip(g_b, 1), False), 1)
    gla = rmsnorm((o_f + o_b).astype(x.dtype), gla_norm_g)
    gla = gla * jax.nn.silu(gr.reshape(B, S, GLA_HEADS, GLA_DV))

    mix = jnp.concatenate([da.reshape(B, S, DA_WIDTH).astype(x.dtype),
                           gla.reshape(B, S, GLA_WIDTH).astype(x.dtype)], axis=-1) @ w_out
    x = x + rmsnorm(mix, post_mix_g)

    h = rmsnorm(x, pre_ffn_g)
    gate, up = jnp.split(h @ w_ffn_up, 2, axis=-1)
    act = jax.nn.gelu(dwconv3(gate, ffn_conv_w, ffn_conv_b)) * up
    x = x + rmsnorm(act @ w_ffn_down, post_ffn_g)

    e = rmsnorm(p.astype(x.dtype) @ w_ple_proj, ple_norm_g)
    x = x + jax.nn.sigmoid(x @ w_ple_gate + b_ple_gate) * e
    return x


def setup_inputs(seed: int = 0) -> dict:
    key = jax.random.key(seed)
    ks = iter(jax.random.split(key, 40))
    f32 = jnp.float32

    def nrm(shape, scale):
        return jax.random.normal(next(ks), shape, f32) * scale

    def gain(n):
        return 1.0 + nrm((DEPTH, n), 0.02)

    L = DEPTH
    return {
        'x_prompt': nrm((BATCH, SEQ, D_MODEL), 1.0),
        'x_sample': nrm((DEC_BATCH, DEC_SEQ, D_MODEL), 1.0),
        'p_prompt': nrm((DEPTH, BATCH, SEQ, PLE_DIM), 1.0),
        'p_sample': nrm((DEPTH, DEC_BATCH, DEC_SEQ, PLE_DIM), 1.0),
        'pre_mix_g': gain(D_MODEL),
        'w_in': nrm((L, D_MODEL, IN_WIDTH), D_MODEL ** -0.5),
        'da_lq1': nrm((L, DA_QK_DIM), 0.1),
        'da_lk1': nrm((L, DA_QK_DIM), 0.1),
        'da_lq2': nrm((L, DA_QK_DIM), 0.1),
        'da_lk2': nrm((L, DA_QK_DIM), 0.1),
        'da_norm_g': gain(DA_V_DIM),
        'gla_w_gate_f': nrm((L, GLA_GATE_RANK, GLA_HEADS * GLA_DK), GLA_GATE_RANK ** -0.5),
        'gla_b_gate_f': nrm((L, GLA_HEADS * GLA_DK), 0.1),
        'gla_w_gate_b': nrm((L, GLA_GATE_RANK, GLA_HEADS * GLA_DK), GLA_GATE_RANK ** -0.5),
        'gla_b_gate_b': nrm((L, GLA_HEADS * GLA_DK), 0.1),
        'gla_norm_g': gain(GLA_DV),
        'w_out': nrm((L, MIX_WIDTH, D_MODEL), MIX_WIDTH ** -0.5),
        'post_mix_g': gain(D_MODEL),
        'pre_ffn_g': gain(D_MODEL),
        'w_ffn_up': nrm((L, D_MODEL, 2 * D_FF), D_MODEL ** -0.5),
        'ffn_conv_w': nrm((L, CONV_WIDTH, D_FF), CONV_WIDTH ** -0.5),
        'ffn_conv_b': nrm((L, D_FF), 0.01),
        'w_ffn_down': nrm((L, D_FF, D_MODEL), D_FF ** -0.5),
        'post_ffn_g': gain(D_MODEL),
        'w_ple_gate': nrm((L, D_MODEL, D_MODEL), D_MODEL ** -0.5),
        'b_ple_gate': nrm((L, D_MODEL), 0.01),
        'w_ple_proj': nrm((L, PLE_DIM, D_MODEL), PLE_DIM ** -0.5),
        'ple_norm_g': gain(D_MODEL),
    }


def reference(x_prompt, x_sample, p_prompt, p_sample, pre_mix_g, w_in, da_lq1, da_lk1,
              da_lq2, da_lk2, da_norm_g, gla_w_gate_f, gla_b_gate_f, gla_w_gate_b,
              gla_b_gate_b, gla_norm_g, w_out, post_mix_g, pre_ffn_g, w_ffn_up,
              ffn_conv_w, ffn_conv_b, w_ffn_down, post_ffn_g, w_ple_gate, b_ple_gate,
              w_ple_proj, ple_norm_g):
    def trunk(x, p):
        for l in range(DEPTH):
            lam_init = 0.8 - 0.6 * math.exp(-0.3 * l)
            x = encoder_layer(x, p[l], lam_init, pre_mix_g[l], w_in[l], da_lq1[l],
                              da_lk1[l], da_lq2[l], da_lk2[l], da_norm_g[l],
                              gla_w_gate_f[l], gla_b_gate_f[l], gla_w_gate_b[l],
                              gla_b_gate_b[l], gla_norm_g[l], w_out[l], post_mix_g[l],
                              pre_ffn_g[l], w_ffn_up[l], ffn_conv_w[l], ffn_conv_b[l],
                              w_ffn_down[l], post_ffn_g[l], w_ple_gate[l], b_ple_gate[l],
                              w_ple_proj[l], ple_norm_g[l])
        return x

    y_prompt = trunk(x_prompt, p_prompt)
    y_sample = trunk(x_sample, p_sample)
    return (y_prompt, y_sample)
```

```python
import functools
import math

import jax
import jax.numpy as jnp
from jax import lax
from jax.experimental import pallas as pl
from jax.experimental.pallas import tpu as pltpu

F32 = jnp.float32
BF16 = jnp.bfloat16

D_MODEL = 1024
PLE_DIM = 256
DA_HEADS = 4
DA_QK_DIM = 64
DA_V_DIM = 128
DA_WIDTH = DA_HEADS * DA_V_DIM
GLA_HEADS = 4
GLA_DK = 64
GLA_DV = 128
GLA_QK_WIDTH = GLA_HEADS * GLA_DK
GLA_WIDTH = GLA_HEADS * GLA_DV
GLA_GATE_RANK = 16
GLA_GATE_NORM = 16.0
GLA_CHUNK = 64
D_FF = 4 * D_MODEL
NORM_EPS = 1e-6
LANES = 128
MAIN_WIDTH = 3 * DA_WIDTH + 2 * GLA_QK_WIDTH + 2 * GLA_WIDTH
NEG_BIG = -1e30
VMEM_LIMIT = 56 * 1024 * 1024

COL_DA_Q = 0
COL_DA_K = DA_WIDTH
COL_DA_V = 2 * DA_WIDTH
COL_GQ = 3 * DA_WIDTH
COL_GK = COL_GQ + GLA_QK_WIDTH
COL_GV = COL_GK + GLA_QK_WIDTH
COL_GR = COL_GV + GLA_WIDTH


def _rms(x, g):
    return x * lax.rsqrt(jnp.mean(x * x, axis=-1, keepdims=True) + NORM_EPS) * g


def _nt_dot(a, b):
    return lax.dot_general(a, b, (((1,), (1,)), ((), ())), preferred_element_type=F32)


def _tn_dot(a, b):
    return lax.dot_general(a, b, (((0,), (0,)), ((), ())), preferred_element_type=F32)


def _dot(a, b):
    return jnp.dot(a, b, preferred_element_type=F32)


def _in_proj_body(x_ref, g_ref, wm_ref, wlr_ref, proj_ref, kt_ref, lr_ref):
    h = _rms(x_ref[...], g_ref[...]).astype(BF16)
    main = _dot(h, wm_ref[...])
    proj_ref[...] = main.astype(BF16)
    kt_ref[...] = main[:, COL_DA_K:COL_DA_K + DA_WIDTH].T.astype(BF16)
    lr_ref[...] = _dot(h, wlr_ref[...]).astype(BF16)


def _in_proj(x2d, g, w_main, w_lr, batch, seq, tm):
    t = x2d.shape[0]
    per_seq = seq // tm
    return pl.pallas_call(
        _in_proj_body,
        grid=(t // tm,),
        in_specs=[
            pl.BlockSpec((tm, D_MODEL), lambda i: (i, 0)),
            pl.BlockSpec((1, D_MODEL), lambda i: (0, 0)),
            pl.BlockSpec((D_MODEL, MAIN_WIDTH), lambda i: (0, 0)),
            pl.BlockSpec((D_MODEL, LANES), lambda i: (0, 0)),
        ],
        out_specs=[
            pl.BlockSpec((tm, MAIN_WIDTH), lambda i: (i, 0)),
            pl.BlockSpec((None, DA_WIDTH, tm), lambda i: (i // per_seq, 0, i % per_seq)),
            pl.BlockSpec((tm, LANES), lambda i: (i, 0)),
        ],
        out_shape=[
            jax.ShapeDtypeStruct((t, MAIN_WIDTH), BF16),
            jax.ShapeDtypeStruct((batch, DA_WIDTH, seq), BF16),
            jax.ShapeDtypeStruct((t, LANES), BF16),
        ],
        compiler_params=pltpu.CompilerParams(
            dimension_semantics=("arbitrary",), vmem_limit_bytes=VMEM_LIMIT),
        name="in_proj",
    )(x2d, g, w_main, w_lr)


def _attn_consts(tq):
    i = jnp.arange(tq, dtype=jnp.int32)
    lo = (i & 255).astype(F32)
    hi = (i - (i & 255)).astype(F32)
    slopes = jnp.asarray([2.0 ** (-8.0 * (h + 1) / DA_HEADS) for h in range(DA_HEADS)], F32)
    qcols = jnp.stack([lo, hi, jnp.ones_like(lo), jnp.ones_like(lo)], axis=-1)
    qcols = slopes[:, None, None] * qcols[None]
    half = jnp.zeros((DA_HEADS, tq, DA_QK_DIM), F32).at[:, :, :4].set(qcols)
    zeros = jnp.zeros_like(half)
    augq = jnp.stack([jnp.concatenate([zeros, half], -1),
                      jnp.concatenate([half, zeros], -1)], axis=1).astype(BF16)
    ones = jnp.ones_like(lo)
    krows = jnp.stack([-ones, -ones, lo, hi], axis=0)
    left = jnp.zeros((DA_QK_DIM, tq), F32).at[:4].set(krows)
    augk = jnp.stack([left, -left], axis=0).astype(BF16)
    d = (i[:, None] - i[None, :]).astype(F32)
    corr = 2.0 * slopes[:, None, None] * jnp.minimum(d, 0.0)[None]
    onescol = jnp.zeros((tq, LANES), F32).at[:, 0].set(1.0).astype(BF16)
    return augq, augk, corr, onescol


def _attn_body(lam_ref, q_ref, kt_ref, v_ref, augq_ref, augk_ref, corr_ref, ones_ref, gn_ref,
               o_ref, qa_ref, acc_ref, m_ref, *, tq, nk, lam_init):
    h = pl.program_id(1)
    qi = pl.program_id(2)
    slope = jnp.where(h == 0, 2.0 ** -2, jnp.where(h == 1, 2.0 ** -4,
                      jnp.where(h == 2, 2.0 ** -6, 2.0 ** -8))).astype(F32)

    q = q_ref[...]
    lane = lax.broadcasted_iota(jnp.int32, (tq, LANES), 1)
    qa_ref[0] = jnp.where(lane < DA_QK_DIM, q, augq_ref[0])
    qa_ref[1] = jnp.where(lane >= DA_QK_DIM, q, augq_ref[1])
    m_ref[...] = jnp.full(m_ref.shape, NEG_BIG, F32)
    acc_ref[...] = jnp.zeros(acc_ref.shape, F32)

    def tile(kt, side, diag):
        off = pl.multiple_of(kt * tq, tq)
        k_t = kt_ref[:, pl.ds(off, tq)]
        v = v_ref[pl.ds(off, tq), :]
        ak = augk_ref[side]
        s0 = _dot(qa_ref[0], jnp.concatenate([k_t[:DA_QK_DIM], ak], axis=0))
        s1 = _dot(qa_ref[1], jnp.concatenate([ak, k_t[DA_QK_DIM:]], axis=0))
        if diag:
            s0 = s0 + corr_ref[...]
            s1 = s1 + corr_ref[...]
            c = jnp.zeros((), F32)
        else:
            c = -slope * (jnp.abs(qi - kt) * tq).astype(F32)
        s = jnp.concatenate([s0, s1], axis=0)
        m_old = m_ref[...]
        m_new = jnp.maximum(m_old, jnp.max(s, axis=-1, keepdims=True) + c)
        alpha = jnp.exp(m_old - m_new)
        p = jnp.exp(s - (m_new - c)).astype(BF16)
        pv = _dot(p, jnp.concatenate([v, ones_ref[...]], axis=1))
        acc_ref[...] = alpha * acc_ref[...] + pv
        m_ref[...] = m_new

    tile(qi, 0, True)

    def left_body(kt, carry):
        tile(kt, 0, False)
        return carry

    def right_body(kt, carry):
        tile(kt, 1, False)
        return carry

    lax.fori_loop(0, qi, left_body, 0)
    lax.fori_loop(qi + 1, nk, right_body, 0)

    lv = lam_ref[...]
    lam = (jnp.exp(jnp.sum(lv[0:1] * lv[1:2], axis=-1, keepdims=True))
           - jnp.exp(jnp.sum(lv[2:3] * lv[3:4], axis=-1, keepdims=True)) + lam_init)
    acc = acc_ref[...]
    o0 = acc[:tq, :DA_V_DIM] / acc[:tq, DA_V_DIM:DA_V_DIM + 1]
    o1 = acc[tq:, :DA_V_DIM] / acc[tq:, DA_V_DIM:DA_V_DIM + 1]
    out = _rms(o0 - lam * o1, gn_ref[...]) * (1.0 - lam_init)
    o_ref[...] = out.astype(o_ref.dtype)


def _diff_attention(proj, k_t, lamvec, gn, batch, seq, tq, lam_init):
    t = proj.shape[0]
    nk = seq // tq
    augq, augk, corr, onescol = _attn_consts(tq)
    body = functools.partial(_attn_body, tq=tq, nk=nk, lam_init=lam_init)
    qblk = COL_DA_Q // LANES
    vblk = COL_DA_V // LANES
    return pl.pallas_call(
        body,
        grid=(batch, DA_HEADS, nk),
        in_specs=[
            pl.BlockSpec((8, LANES), lambda b, h, i: (0, 0)),
            pl.BlockSpec((tq, LANES), lambda b, h, i: (b * nk + i, qblk + h)),
            pl.BlockSpec((None, LANES, seq), lambda b, h, i: (b, h, 0)),
            pl.BlockSpec((seq, LANES), lambda b, h, i: (b, vblk + h)),
            pl.BlockSpec((None, 2, tq, LANES), lambda b, h, i: (h, 0, 0, 0)),
            pl.BlockSpec((2, DA_QK_DIM, tq), lambda b, h, i: (0, 0, 0)),
            pl.BlockSpec((None, tq, tq), lambda b, h, i: (h, 0, 0)),
            pl.BlockSpec((tq, LANES), lambda b, h, i: (0, 0)),
            pl.BlockSpec((1, DA_V_DIM), lambda b, h, i: (0, 0)),
        ],
        out_specs=pl.BlockSpec((tq, LANES), lambda b, h, i: (b * nk + i, h)),
        out_shape=jax.ShapeDtypeStruct((t, DA_WIDTH), BF16),
        scratch_shapes=[
            pltpu.VMEM((2, tq, LANES), BF16),
            pltpu.VMEM((2 * tq, 2 * LANES), F32),
            pltpu.VMEM((2 * tq, 1), F32),
        ],
        compiler_params=pltpu.CompilerParams(
            dimension_semantics=("arbitrary", "arbitrary", "arbitrary"),
            vmem_limit_bytes=VMEM_LIMIT),
        name="diff_attention",
    )(lamvec, proj, k_t, proj, augq, augk, corr, onescol, gn)


def _log_sigmoid(z):
    return jnp.minimum(z, 0.0) - jnp.log1p(jnp.exp(-jnp.abs(z)))


def _gla_chunk(q, k, v, g, st_ref, reverse):
    c = GLA_CHUNK
    q = q.astype(F32)
    k = k.astype(F32)
    row = lax.broadcasted_iota(jnp.int32, (c, GLA_QK_WIDTH), 0)
    lane = lax.broadcasted_iota(jnp.int32, (c, GLA_QK_WIDTH), 1)
    b = g
    shift = 1
    while shift < c:
        b = b + jnp.where(row >= shift, pltpu.roll(b, shift, 0), 0.0)
        shift *= 2
    tot = b[c - 1:c, :]
    if reverse:
        b = tot - b + g
    q_t = (q * jnp.exp(b)).astype(BF16)
    k_t = (k * jnp.exp(-b)).astype(BF16)
    k_end = (k * jnp.exp(tot - b)).astype(BF16)
    decay = jnp.exp(tot)

    head = lane // GLA_DK
    zero_k = jnp.zeros_like(k_t)
    k_bd = jnp.concatenate([jnp.where(head == h, k_t, zero_k) for h in range(GLA_HEADS)], axis=0)
    att = _nt_dot(q_t, k_bd)
    pos = lane % c
    keep = (pos > row) if reverse else (pos <= row)
    att = jnp.where(keep, att, 0.0).astype(BF16)

    vhead = lax.broadcasted_iota(jnp.int32, (c, GLA_WIDTH), 1) // GLA_DV
    zero_v = jnp.zeros_like(v)
    v_bd = jnp.concatenate([jnp.where(vhead == h, v, zero_v) for h in range(GLA_HEADS)], axis=0)
    st = st_ref[...]
    out = _dot(att, v_bd) + _nt_dot(q_t, st.astype(BF16))

    kv_t = _tn_dot(v, k_end)
    r_head = lax.broadcasted_iota(jnp.int32, kv_t.shape, 0) // GLA_DV
    c_head = lax.broadcasted_iota(jnp.int32, kv_t.shape, 1) // GLA_DK
    st_ref[...] = decay * st + jnp.where(r_head == c_head, kv_t, 0.0)
    return out


def _gla_body(qf_ref, kf_ref, vf_ref, lrf_ref, qb_ref, kb_ref, vb_ref, lrb_ref, wg_ref, bg_ref,
              of_ref, ob_ref, stf_ref, stb_ref, *, rows):
    @pl.when(pl.program_id(1) == 0)
    def _():
        stf_ref[...] = jnp.zeros(stf_ref.shape, F32)
        stb_ref[...] = jnp.zeros(stb_ref.shape, F32)

    wg = wg_ref[...]
    bg = bg_ref[...]
    zf = _dot(lrf_ref[...], wg[:, :GLA_QK_WIDTH]) + bg[:, :GLA_QK_WIDTH]
    zb = _dot(lrb_ref[...], wg[:, GLA_QK_WIDTH:]) + bg[:, GLA_QK_WIDTH:]
    gf = _log_sigmoid(zf) / GLA_GATE_NORM
    gb = _log_sigmoid(zb) / GLA_GATE_NORM
    nc = rows // GLA_CHUNK
    for ci in range(nc):
        sf = slice(ci * GLA_CHUNK, (ci + 1) * GLA_CHUNK)
        of_ref[sf, :] = _gla_chunk(qf_ref[sf, :], kf_ref[sf, :], vf_ref[sf, :], gf[sf, :],
                                   stf_ref, False)
        cj = nc - 1 - ci
        sb = slice(cj * GLA_CHUNK, (cj + 1) * GLA_CHUNK)
        ob_ref[sb, :] = _gla_chunk(qb_ref[sb, :], kb_ref[sb, :], vb_ref[sb, :], gb[sb, :],
                                   stb_ref, True)


def _gla(proj, lr, wg, bg, batch, seq, rows):
    t = proj.shape[0]
    nb = seq // rows
    qblk = COL_GQ // GLA_QK_WIDTH
    kblk = COL_GK // GLA_QK_WIDTH
    vblk = COL_GV // GLA_WIDTH

    def fwd(col):
        return lambda b, i: (b * nb + i, col)

    def bwd(col):
        return lambda b, i: (b * nb + nb - 1 - i, col)

    def specs(ix):
        return [
            pl.BlockSpec((rows, GLA_QK_WIDTH), ix(qblk)),
            pl.BlockSpec((rows, GLA_QK_WIDTH), ix(kblk)),
            pl.BlockSpec((rows, GLA_WIDTH), ix(vblk)),
            pl.BlockSpec((rows, LANES), ix(0)),
        ]

    return pl.pallas_call(
        functools.partial(_gla_body, rows=rows),
        grid=(batch, nb),
        in_specs=specs(fwd) + specs(bwd) + [
            pl.BlockSpec((LANES, 2 * GLA_QK_WIDTH), lambda b, i: (0, 0)),
            pl.BlockSpec((1, 2 * GLA_QK_WIDTH), lambda b, i: (0, 0)),
        ],
        out_specs=[
            pl.BlockSpec((rows, GLA_WIDTH), fwd(0)),
            pl.BlockSpec((rows, GLA_WIDTH), bwd(0)),
        ],
        out_shape=[jax.ShapeDtypeStruct((t, GLA_WIDTH), F32)] * 2,
        scratch_shapes=[pltpu.VMEM((GLA_WIDTH, GLA_QK_WIDTH), F32)] * 2,
        compiler_params=pltpu.CompilerParams(
            dimension_semantics=("arbitrary", "arbitrary"), vmem_limit_bytes=VMEM_LIMIT),
        name="gla",
    )(proj, proj, proj, lr, proj, proj, proj, lr, wg, bg)


def _out_proj_body(da_ref, of_ref, ob_ref, gr_ref, x_ref, gng_ref, wout_ref, pmg_ref, pfg_ref,
                   x1_ref, h2_ref):
    o = of_ref[...] + ob_ref[...]
    gr = gr_ref[...].astype(F32)
    parts = []
    for h in range(GLA_HEADS):
        sl = slice(h * GLA_DV, (h + 1) * GLA_DV)
        gh = gr[:, sl]
        parts.append(_rms(o[:, sl], gng_ref[...]) * (gh * jax.nn.sigmoid(gh)))
    gla = jnp.concatenate(parts, axis=1).astype(BF16)
    mix = _dot(jnp.concatenate([da_ref[...], gla], axis=1), wout_ref[...])
    x1 = x_ref[...] + _rms(mix, pmg_ref[...])
    x1_ref[...] = x1
    h2_ref[...] = _rms(x1, pfg_ref[...]).astype(BF16)


def _out_proj(da, o_f, o_b, proj, x2d, gng, w_out, pmg, pfg, tm):
    t = x2d.shape[0]
    row = lambda i: (i, 0)
    const = lambda i: (0, 0)
    return pl.pallas_call(
        _out_proj_body,
        grid=(t // tm,),
        in_specs=[
            pl.BlockSpec((tm, DA_WIDTH), row),
            pl.BlockSpec((tm, GLA_WIDTH), row),
            pl.BlockSpec((tm, GLA_WIDTH), row),
            pl.BlockSpec((tm, GLA_WIDTH), lambda i: (i, COL_GR // GLA_WIDTH)),
            pl.BlockSpec((tm, D_MODEL), row),
            pl.BlockSpec((1, GLA_DV), const),
            pl.BlockSpec((D_MODEL, D_MODEL), const),
            pl.BlockSpec((1, D_MODEL), const),
            pl.BlockSpec((1, D_MODEL), const),
        ],
        out_specs=[pl.BlockSpec((tm, D_MODEL), row), pl.BlockSpec((tm, D_MODEL), row)],
        out_shape=[jax.ShapeDtypeStruct((t, D_MODEL), F32),
                   jax.ShapeDtypeStruct((t, D_MODEL), BF16)],
        compiler_params=pltpu.CompilerParams(
            dimension_semantics=("arbitrary",), vmem_limit_bytes=VMEM_LIMIT),
        name="out_proj",
    )(da, o_f, o_b, proj, x2d, gng, w_out, pmg, pfg)


HALO = 16


def _ffn_body(h_ref, hp_ref, hn_ref, wg_ref, wu_ref, cw_ref, cb_ref, wd_ref, x1_ref, p_ref,
              pfg_ref, wpg_ref, bpg_ref, wpp_ref, png_ref, o_ref, hext_ref, acc_ref,
              *, tm, seq, nff):
    i = pl.program_id(0)
    j = pl.program_id(1)

    @pl.when(j == 0)
    def _():
        first = (i * tm) % seq == 0
        last = ((i + 1) * tm) % seq == 0
        hp = hp_ref[...]
        hn = hn_ref[...]
        hext_ref[0:HALO, :] = jnp.where(first, jnp.zeros_like(hp), hp)
        hext_ref[HALO:HALO + tm, :] = h_ref[...]
        hext_ref[HALO + tm:, :] = jnp.where(last, jnp.zeros_like(hn), hn)

    gate = _dot(hext_ref[...], wg_ref[...])
    ext = tm + 2 * HALO
    cw = cw_ref[...]
    conv = (pltpu.roll(gate, 1, 0)[HALO:HALO + tm] * cw[0:1]
            + gate[HALO:HALO + tm] * cw[1:2]
            + pltpu.roll(gate, ext - 1, 0)[HALO:HALO + tm] * cw[2:3]
            + cb_ref[...])
    up = _dot(h_ref[...], wu_ref[...])
    act = (jax.nn.gelu(conv) * up).astype(BF16)
    part = _dot(act, wd_ref[...])

    @pl.when(j == 0)
    def _():
        acc_ref[...] = part

    @pl.when(j > 0)
    def _():
        acc_ref[...] += part

    @pl.when(j == nff - 1)
    def _():
        x2 = x1_ref[...] + _rms(acc_ref[...], pfg_ref[...])
        e = _rms(_dot(p_ref[...].astype(BF16), wpp_ref[...]), png_ref[...])
        gate_e = jax.nn.sigmoid(_dot(x2.astype(BF16), wpg_ref[...]) + bpg_ref[...])
        o_ref[...] = x2 + gate_e * e


def _ffn(h2, x1, p2d, w_up, conv_w, conv_b, w_down, pfg, w_pg, b_pg, w_pp, png, seq, tm, tf):
    t = h2.shape[0]
    nff = D_FF // tf
    nhalo = t // HALO
    per = tm // HALO
    row = lambda i, j: (i, 0)
    const = lambda i, j: (0, 0)
    return pl.pallas_call(
        functools.partial(_ffn_body, tm=tm, seq=seq, nff=nff),
        grid=(t // tm, nff),
        in_specs=[
            pl.BlockSpec((tm, D_MODEL), row),
            pl.BlockSpec((HALO, D_MODEL), lambda i, j: (jnp.maximum(i * per - 1, 0), 0)),
            pl.BlockSpec((HALO, D_MODEL), lambda i, j: (jnp.minimum((i + 1) * per, nhalo - 1), 0)),
            pl.BlockSpec((D_MODEL, tf), lambda i, j: (0, j)),
            pl.BlockSpec((D_MODEL, tf), lambda i, j: (0, nff + j)),
            pl.BlockSpec((3, tf), lambda i, j: (0, j)),
            pl.BlockSpec((1, tf), lambda i, j: (0, j)),
            pl.BlockSpec((tf, D_MODEL), lambda i, j: (j, 0)),
            pl.BlockSpec((tm, D_MODEL), row),
            pl.BlockSpec((tm, PLE_DIM), row),
            pl.BlockSpec((1, D_MODEL), const),
            pl.BlockSpec((D_MODEL, D_MODEL), const),
            pl.BlockSpec((1, D_MODEL), const),
            pl.BlockSpec((PLE_DIM, D_MODEL), const),
            pl.BlockSpec((1, D_MODEL), const),
        ],
        out_specs=pl.BlockSpec((tm, D_MODEL), row),
        out_shape=jax.ShapeDtypeStruct((t, D_MODEL), F32),
        scratch_shapes=[
            pltpu.VMEM((tm + 2 * HALO, D_MODEL), BF16),
            pltpu.VMEM((tm, D_MODEL), F32),
        ],
        compiler_params=pltpu.CompilerParams(
            dimension_semantics=("arbitrary", "arbitrary"), vmem_limit_bytes=VMEM_LIMIT),
        name="conv_ffn",
    )(h2, h2, h2, w_up, w_up, conv_w, conv_b, w_down, x1, p2d, pfg, w_pg, b_pg, w_pp, png)


def _prep_weights(l, w_in, gla_w_gate_f, gla_b_gate_f, gla_w_gate_b, gla_b_gate_b, da_lq1,
                  da_lk1, da_lq2, da_lk2):
    w = w_in[l]
    qscale = jnp.concatenate([
        jnp.full((DA_WIDTH,), DA_QK_DIM ** -0.5, F32), jnp.ones((2 * DA_WIDTH,), F32),
        jnp.full((GLA_QK_WIDTH,), GLA_DK ** -0.5, F32),
        jnp.ones((GLA_QK_WIDTH + 2 * GLA_WIDTH,), F32)])
    w_main = (w[:, :MAIN_WIDTH] * qscale).astype(BF16)
    w_lr = jnp.zeros((D_MODEL, LANES), F32).at[:, :2 * GLA_GATE_RANK].set(w[:, MAIN_WIDTH:])
    wg = jnp.zeros((LANES, 2 * GLA_QK_WIDTH), F32)
    wg = wg.at[:GLA_GATE_RANK, :GLA_QK_WIDTH].set(gla_w_gate_f[l])
    wg = wg.at[GLA_GATE_RANK:2 * GLA_GATE_RANK, GLA_QK_WIDTH:].set(gla_w_gate_b[l])
    bg = jnp.concatenate([gla_b_gate_f[l], gla_b_gate_b[l]])[None, :]
    lamvec = jnp.zeros((8, LANES), F32)
    for r, vec in enumerate((da_lq1, da_lk1, da_lq2, da_lk2)):
        lamvec = lamvec.at[r, :DA_QK_DIM].set(vec[l].astype(F32))
    return w_main, w_lr.astype(BF16), wg.astype(BF16), bg, lamvec


ROW_TILE = 512
ATTN_TILE = 512
GLA_ROWS = 256
FF_TILE = 512


def _layer(x2d, p2d, batch, seq, lam_init, wts):
    tm = min(seq, ROW_TILE)
    proj, k_t, lr = _in_proj(x2d, wts["pre_mix_g"], wts["w_main"], wts["w_lr"], batch, seq, tm)
    da = _diff_attention(proj, k_t, wts["lamvec"], wts["da_norm_g"], batch, seq,
                         min(seq, ATTN_TILE), lam_init)
    o_f, o_b = _gla(proj, lr, wts["wg"], wts["bg"], batch, seq, min(seq, GLA_ROWS))
    x1, h2 = _out_proj(da, o_f, o_b, proj, x2d, wts["gla_norm_g"], wts["w_out"],
                       wts["post_mix_g"], wts["pre_ffn_g"], tm)
    return _ffn(h2, x1, p2d, wts["w_ffn_up"], wts["ffn_conv_w"], wts["ffn_conv_b"],
                wts["w_ffn_down"], wts["post_ffn_g"], wts["w_ple_gate"], wts["b_ple_gate"],
                wts["w_ple_proj"], wts["ple_norm_g"], seq, tm, FF_TILE)


def kernel(x_prompt, x_sample, p_prompt, p_sample, pre_mix_g, w_in, da_lq1, da_lk1, da_lq2, da_lk2, da_norm_g, gla_w_gate_f, gla_b_gate_f, gla_w_gate_b, gla_b_gate_b, gla_norm_g, w_out, post_mix_g, pre_ffn_g, w_ffn_up, ffn_conv_w, ffn_conv_b, w_ffn_down, post_ffn_g, w_ple_gate, b_ple_gate, w_ple_proj, ple_norm_g):
    depth = w_in.shape[0]
    layers = []
    for l in range(depth):
        w_main, w_lr, wg, bg, lamvec = _prep_weights(
            l, w_in, gla_w_gate_f, gla_b_gate_f, gla_w_gate_b, gla_b_gate_b,
            da_lq1, da_lk1, da_lq2, da_lk2)
        row = lambda a: a[l][None, :].astype(F32)
        layers.append(dict(
            w_main=w_main, w_lr=w_lr, wg=wg, bg=bg, lamvec=lamvec,
            pre_mix_g=row(pre_mix_g), da_norm_g=row(da_norm_g), gla_norm_g=row(gla_norm_g),
            w_out=w_out[l].astype(BF16), post_mix_g=row(post_mix_g), pre_ffn_g=row(pre_ffn_g),
            w_ffn_up=w_ffn_up[l].astype(BF16), ffn_conv_w=ffn_conv_w[l].astype(F32),
            ffn_conv_b=row(ffn_conv_b), w_ffn_down=w_ffn_down[l].astype(BF16),
            post_ffn_g=row(post_ffn_g), w_ple_gate=w_ple_gate[l].astype(BF16),
            b_ple_gate=row(b_ple_gate), w_ple_proj=w_ple_proj[l].astype(BF16),
            ple_norm_g=row(ple_norm_g)))

    def trunk(x, p):
        batch, seq, _ = x.shape
        t = batch * seq
        x2d = x.reshape(t, D_MODEL)
        for l in range(depth):
            lam_init = 0.8 - 0.6 * math.exp(-0.3 * l)
            x2d = _layer(x2d, p[l].reshape(t, PLE_DIM), batch, seq, lam_init, layers[l])
        return x2d.reshape(batch, seq, D_MODEL)

    return (trunk(x_prompt, p_prompt), trunk(x_sample, p_sample))
```

```python
import functools
import math

import jax
import jax.numpy as jnp
from jax import lax
from jax.experimental import pallas as pl
from jax.experimental.pallas import tpu as pltpu

F32 = jnp.float32
BF16 = jnp.bfloat16

D_MODEL = 1024
PLE_DIM = 256
DA_HEADS = 4
DA_QK_DIM = 64
DA_V_DIM = 128
DA_WIDTH = DA_HEADS * DA_V_DIM
GLA_HEADS = 4
GLA_DK = 64
GLA_DV = 128
GLA_QK_WIDTH = GLA_HEADS * GLA_DK
GLA_WIDTH = GLA_HEADS * GLA_DV
GLA_GATE_RANK = 16
GLA_GATE_NORM = 16.0
GLA_CHUNK = 64
D_FF = 4 * D_MODEL
NORM_EPS = 1e-6
LANES = 128
MAIN_WIDTH = 3 * DA_WIDTH + 2 * GLA_QK_WIDTH + 2 * GLA_WIDTH
NEG_BIG = -1e30
LOG2E = math.log2(math.e)
ONES_ROWS = 16
VMEM_LIMIT = 56 * 1024 * 1024

COL_DA_Q = 0
COL_DA_K = DA_WIDTH
COL_DA_V = 2 * DA_WIDTH
COL_GQ = 3 * DA_WIDTH
COL_GK = COL_GQ + GLA_QK_WIDTH
COL_GV = COL_GK + GLA_QK_WIDTH
COL_GR = COL_GV + GLA_WIDTH


def _rms(x, g):
    return x * lax.rsqrt(jnp.mean(x * x, axis=-1, keepdims=True) + NORM_EPS) * g


def _nt_dot(a, b):
    return lax.dot_general(a, b, (((1,), (1,)), ((), ())), preferred_element_type=F32)


def _tn_dot(a, b):
    return lax.dot_general(a, b, (((0,), (0,)), ((), ())), preferred_element_type=F32)


def _dot(a, b):
    return jnp.dot(a, b, preferred_element_type=F32)


def _in_proj_body(x_ref, g_ref, wm_ref, wlr_ref, proj_ref, qt_ref, vt_ref, lr_ref):
    h = _rms(x_ref[...], g_ref[...]).astype(BF16)
    main = _dot(h, wm_ref[...])
    proj_ref[...] = main.astype(BF16)
    qt_ref[...] = main[:, COL_DA_Q:COL_DA_Q + DA_WIDTH].T.astype(BF16)
    for hd in range(DA_HEADS):
        c0 = COL_DA_V + hd * DA_V_DIM
        vt_ref[hd, :DA_V_DIM, :] = main[:, c0:c0 + DA_V_DIM].T.astype(BF16)
        vt_ref[hd, DA_V_DIM:, :] = jnp.ones((ONES_ROWS, main.shape[0]), BF16)
    lr_ref[...] = _dot(h, wlr_ref[...]).astype(BF16)


def _in_proj(x2d, g, w_main, w_lr, batch, seq, tm):
    t = x2d.shape[0]
    per_seq = seq // tm
    va_rows = DA_V_DIM + ONES_ROWS
    return pl.pallas_call(
        _in_proj_body,
        grid=(t // tm,),
        in_specs=[
            pl.BlockSpec((tm, D_MODEL), lambda i: (i, 0)),
            pl.BlockSpec((1, D_MODEL), lambda i: (0, 0)),
            pl.BlockSpec((D_MODEL, MAIN_WIDTH), lambda i: (0, 0)),
            pl.BlockSpec((D_MODEL, LANES), lambda i: (0, 0)),
        ],
        out_specs=[
            pl.BlockSpec((tm, MAIN_WIDTH), lambda i: (i, 0)),
            pl.BlockSpec((None, DA_WIDTH, tm), lambda i: (i // per_seq, 0, i % per_seq)),
            pl.BlockSpec((None, DA_HEADS, va_rows, tm),
                         lambda i: (i // per_seq, 0, 0, i % per_seq)),
            pl.BlockSpec((tm, LANES), lambda i: (i, 0)),
        ],
        out_shape=[
            jax.ShapeDtypeStruct((t, MAIN_WIDTH), BF16),
            jax.ShapeDtypeStruct((batch, DA_WIDTH, seq), BF16),
            jax.ShapeDtypeStruct((batch, DA_HEADS, va_rows, seq), BF16),
            jax.ShapeDtypeStruct((t, LANES), BF16),
        ],
        compiler_params=pltpu.CompilerParams(
            dimension_semantics=("arbitrary",), vmem_limit_bytes=VMEM_LIMIT),
        name="in_proj",
    )(x2d, g, w_main, w_lr)


def _split3(x):
    hi = x.astype(BF16).astype(F32)
    mid = (x - hi).astype(BF16).astype(F32)
    lo = (x - hi - mid).astype(BF16).astype(F32)
    return hi, mid, lo


def _attn_consts(tq):
    i = jnp.arange(tq, dtype=jnp.int32)
    lo = (i & 255).astype(F32)
    hi = (i - (i & 255)).astype(F32)
    slopes = jnp.asarray([2.0 ** (-8.0 * (h + 1) / DA_HEADS) for h in range(DA_HEADS)], F32) * LOG2E
    pieces = jnp.stack([p for piece in _split3(slopes) for p in (piece, piece)], axis=1)
    half = jnp.zeros((DA_HEADS, DA_QK_DIM, tq), F32).at[:, :6].set(
        jnp.broadcast_to(pieces[:, :, None], (DA_HEADS, 6, tq)))
    zeros = jnp.zeros_like(half)
    left = jnp.stack([jnp.concatenate([zeros, half], 1),
                      jnp.concatenate([half, zeros], 1)], axis=1)
    augq = jnp.stack([left, -left], axis=1).astype(BF16)
    kcols = jnp.stack([lo, hi, lo, hi, lo, hi], axis=-1)
    khalf = jnp.zeros((tq, DA_QK_DIM), F32).at[:, :6].set(kcols)
    kz = jnp.zeros_like(khalf)
    augk = jnp.stack([jnp.concatenate([kz, khalf], -1),
                      jnp.concatenate([khalf, kz], -1)], axis=0).astype(BF16)
    qb = slopes[:, None] * i.astype(F32)[None, :]
    qbias = jnp.stack([-qb, qb], axis=1)[:, :, None, :]
    d = (i[None, :] - i[:, None]).astype(F32)
    corr = 2.0 * slopes[:, None, None] * jnp.minimum(d, 0.0)[None]
    return augq, augk, qbias, corr


def _attn_body(lam_ref, qt_ref, k_ref, vt_ref, augq_ref, augk_ref, qbias_ref, corr_ref, gn_ref, o_ref,
               qa_ref, *scratch, tq, nk, lam_init):
    grab = lambda i: (scratch[i:i + 2], scratch[i + 2:i + 4])
    s_ref, p_ref, cmax_ref, alpha_ref = grab(0), grab(4), grab(8), grab(12)
    acc_ref, m_ref = scratch[16:18], scratch[18:20]
    h = pl.program_id(1)
    qi = pl.program_id(2)
    slope = jnp.where(h == 0, 2.0 ** -2, jnp.where(h == 1, 2.0 ** -4,
                      jnp.where(h == 2, 2.0 ** -6, 2.0 ** -8))).astype(F32) * LOG2E

    qt = qt_ref[...]
    row = lax.broadcasted_iota(jnp.int32, (LANES, tq), 0)
    for side in range(2):
        qa_ref[2 * side] = jnp.where(row < DA_QK_DIM, qt, augq_ref[side, 0])
        qa_ref[2 * side + 1] = jnp.where(row >= DA_QK_DIM, qt, augq_ref[side, 1])
    for mp in range(2):
        m_ref[mp][...] = jnp.full(m_ref[mp].shape, NEG_BIG, F32)
        acc_ref[mp][...] = jnp.zeros(acc_ref[mp].shape, F32)
    lane = lax.broadcasted_iota(jnp.int32, (tq, LANES), 1)

    def key_tile(t):
        return jnp.where(t == 0, qi, jnp.where(t <= qi, t - 1, t))

    def tile_const(t):
        return -slope * (jnp.abs(qi - key_tile(t)) * tq).astype(F32)

    def tile_side(t):
        return (key_tile(t) > qi).astype(jnp.int32)

    def scores(t, slot, diag):
        kt = key_tile(t)
        side = tile_side(t)
        k = k_ref[pl.ds(pl.multiple_of(kt * tq, tq), tq), :]
        ka = (jnp.where(lane < DA_QK_DIM, k, augk_ref[0]),
              jnp.where(lane >= DA_QK_DIM, k, augk_ref[1]))
        for mp in range(2):
            s = _dot(ka[mp], qa_ref[2 * side + mp])
            if diag:
                s = s + corr_ref[...]
            s_ref[slot][mp][...] = s
            cmax_ref[slot][mp][...] = jnp.max(s, axis=0, keepdims=True) + qbias_ref[side]

    def probs(t, slot):
        shift = tile_const(t) + qbias_ref[tile_side(t)]
        for mp in range(2):
            m_old = m_ref[mp][...]
            m_new = jnp.maximum(m_old, cmax_ref[slot][mp][...] + tile_const(t))
            alpha_ref[slot][mp][...] = jnp.exp2(m_old - m_new)
            p_ref[slot][mp][...] = jnp.exp2((s_ref[slot][mp][...] - (m_new - shift)).astype(BF16))
            m_ref[mp][...] = m_new

    def accumulate(t, slot):
        va = vt_ref[:, pl.ds(pl.multiple_of(key_tile(t) * tq, tq), tq)]
        for mp in range(2):
            acc_ref[mp][...] = (alpha_ref[slot][mp][...] * acc_ref[mp][...]
                                + _dot(va, p_ref[slot][mp][...]))

    def step(t, slot):
        probs(t - 1, 1 - slot)
        scores(t, slot, False)
        accumulate(t - 1, 1 - slot)

    def pair(i, carry):
        step(2 * i + 1, 1)
        step(2 * i + 2, 0)
        return carry

    scores(0, 0, True)
    lax.fori_loop(0, (nk - 1) // 2, pair, 0)
    if (nk - 1) % 2:
        step(nk - 1, (nk - 1) % 2)
    probs(nk - 1, (nk - 1) % 2)
    accumulate(nk - 1, (nk - 1) % 2)

    lv = lam_ref[...]
    lam = (jnp.exp(jnp.sum(lv[0:1] * lv[1:2], axis=-1, keepdims=True))
           - jnp.exp(jnp.sum(lv[2:3] * lv[3:4], axis=-1, keepdims=True)) + lam_init)
    a0 = acc_ref[0][...]
    a1 = acc_ref[1][...]
    o0 = a0[:DA_V_DIM] * (1.0 / a0[DA_V_DIM:DA_V_DIM + 1])
    o1 = a1[:DA_V_DIM] * (1.0 / a1[DA_V_DIM:DA_V_DIM + 1])
    out = o0 - lam * o1
    inv = lax.rsqrt(jnp.mean(out * out, axis=0, keepdims=True) + NORM_EPS)
    out = out * inv * gn_ref[...] * (1.0 - lam_init)
    o_ref[...] = out.T.astype(o_ref.dtype)


def _diff_attention(proj, q_t, v_t, lamvec, gn, batch, seq, tq, lam_init):
    t = proj.shape[0]
    nk = seq // tq
    augq, augk, qbias, corr = _attn_consts(tq)
    gnb = jnp.broadcast_to(gn.reshape(DA_V_DIM, 1), (DA_V_DIM, tq))
    body = functools.partial(_attn_body, tq=tq, nk=nk, lam_init=lam_init)
    kblk = COL_DA_K // LANES
    return pl.pallas_call(
        body,
        grid=(batch, DA_HEADS, nk),
        in_specs=[
            pl.BlockSpec((8, LANES), lambda b, h, i: (0, 0)),
            pl.BlockSpec((None, LANES, tq), lambda b, h, i: (b, h, i)),
            pl.BlockSpec((seq, LANES), lambda b, h, i: (b, kblk + h)),
            pl.BlockSpec((None, None, DA_V_DIM + ONES_ROWS, seq), lambda b, h, i: (b, h, 0, 0)),
            pl.BlockSpec((None, 2, 2, LANES, tq), lambda b, h, i: (h, 0, 0, 0, 0)),
            pl.BlockSpec((2, tq, LANES), lambda b, h, i: (0, 0, 0)),
            pl.BlockSpec((None, 2, 1, tq), lambda b, h, i: (h, 0, 0, 0)),
            pl.BlockSpec((None, tq, tq), lambda b, h, i: (h, 0, 0)),
            pl.BlockSpec((DA_V_DIM, tq), lambda b, h, i: (0, 0)),
        ],
        out_specs=pl.BlockSpec((tq, LANES), lambda b, h, i: (b * nk + i, h)),
        out_shape=jax.ShapeDtypeStruct((t, DA_WIDTH), BF16),
        scratch_shapes=[
            pltpu.VMEM((4, LANES, tq), BF16),
        ] + [pltpu.VMEM((tq, tq), F32)] * 4
        + [pltpu.VMEM((tq, tq), BF16)] * 4
        + [pltpu.VMEM((1, tq), F32)] * 4
        + [pltpu.VMEM((1, tq), F32)] * 4
        + [pltpu.VMEM((DA_V_DIM + ONES_ROWS, tq), F32)] * 2
        + [pltpu.VMEM((1, tq), F32)] * 2,
        compiler_params=pltpu.CompilerParams(
            dimension_semantics=("arbitrary", "arbitrary", "arbitrary"),
            vmem_limit_bytes=VMEM_LIMIT),
        name="diff_attention",
    )(lamvec, q_t, proj, v_t, augq, augk, qbias, corr, gnb)


def _log_sigmoid(z):
    return jnp.minimum(z, 0.0) - jnp.log1p(jnp.exp(-jnp.abs(z)))


def _gla_chunk(q, k, v, g, st_ref, reverse):
    c = GLA_CHUNK
    q = q.astype(F32)
    k = k.astype(F32)
    row = lax.broadcasted_iota(jnp.int32, (c, GLA_QK_WIDTH), 0)
    lane = lax.broadcasted_iota(jnp.int32, (c, GLA_QK_WIDTH), 1)
    b = g
    shift = 1
    while shift < c:
        b = b + jnp.where(row >= shift, pltpu.roll(b, shift, 0), 0.0)
        shift *= 2
    tot = b[c - 1:c, :]
    if reverse:
        b = tot - b + g
    q_t = (q * jnp.exp(b)).astype(BF16)
    k_t = (k * jnp.exp(-b)).astype(BF16)
    k_end = (k * jnp.exp(tot - b)).astype(BF16)
    decay = jnp.exp(tot)

    head = lane // GLA_DK
    zero_k = jnp.zeros_like(k_t)
    k_bd = jnp.concatenate([jnp.where(head == h, k_t, zero_k) for h in range(GLA_HEADS)], axis=0)
    att = _nt_dot(q_t, k_bd)
    pos = lane % c
    keep = (pos > row) if reverse else (pos <= row)
    att = jnp.where(keep, att, 0.0).astype(BF16)

    vhead = lax.broadcasted_iota(jnp.int32, (c, GLA_WIDTH), 1) // GLA_DV
    zero_v = jnp.zeros_like(v)
    v_bd = jnp.concatenate([jnp.where(vhead == h, v, zero_v) for h in range(GLA_HEADS)], axis=0)
    st = st_ref[...]
    out = _dot(att, v_bd) + _nt_dot(q_t, st.astype(BF16))

    kv_t = _tn_dot(v, k_end)
    r_head = lax.broadcasted_iota(jnp.int32, kv_t.shape, 0) // GLA_DV
    c_head = lax.broadcasted_iota(jnp.int32, kv_t.shape, 1) // GLA_DK
    st_ref[...] = decay * st + jnp.where(r_head == c_head, kv_t, 0.0)
    return out


def _gla_body(qf_ref, kf_ref, vf_ref, lrf_ref, qb_ref, kb_ref, vb_ref, lrb_ref, wg_ref, bg_ref,
              of_ref, ob_ref, stf_ref, stb_ref, *, rows):
    @pl.when(pl.program_id(1) == 0)
    def _():
        stf_ref[...] = jnp.zeros(stf_ref.shape, F32)
        stb_ref[...] = jnp.zeros(stb_ref.shape, F32)

    wg = wg_ref[...]
    bg = bg_ref[...]
    zf = _dot(lrf_ref[...], wg[:, :GLA_QK_WIDTH]) + bg[:, :GLA_QK_WIDTH]
    zb = _dot(lrb_ref[...], wg[:, GLA_QK_WIDTH:]) + bg[:, GLA_QK_WIDTH:]
    gf = _log_sigmoid(zf) / GLA_GATE_NORM
    gb = _log_sigmoid(zb) / GLA_GATE_NORM
    nc = rows // GLA_CHUNK
    for ci in range(nc):
        sf = slice(ci * GLA_CHUNK, (ci + 1) * GLA_CHUNK)
        of_ref[sf, :] = _gla_chunk(qf_ref[sf, :], kf_ref[sf, :], vf_ref[sf, :], gf[sf, :],
                                   stf_ref, False)
        cj = nc - 1 - ci
        sb = slice(cj * GLA_CHUNK, (cj + 1) * GLA_CHUNK)
        ob_ref[sb, :] = _gla_chunk(qb_ref[sb, :], kb_ref[sb, :], vb_ref[sb, :], gb[sb, :],
                                   stb_ref, True)


def _gla(proj, lr, wg, bg, batch, seq, rows):
    t = proj.shape[0]
    nb = seq // rows
    qblk = COL_GQ // GLA_QK_WIDTH
    kblk = COL_GK // GLA_QK_WIDTH
    vblk = COL_GV // GLA_WIDTH

    def fwd(col):
        return lambda b, i: (b * nb + i, col)

    def bwd(col):
        return lambda b, i: (b * nb + nb - 1 - i, col)

    def specs(ix):
        return [
            pl.BlockSpec((rows, GLA_QK_WIDTH), ix(qblk)),
            pl.BlockSpec((rows, GLA_QK_WIDTH), ix(kblk)),
            pl.BlockSpec((rows, GLA_WIDTH), ix(vblk)),
            pl.BlockSpec((rows, LANES), ix(0)),
        ]

    return pl.pallas_call(
        functools.partial(_gla_body, rows=rows),
        grid=(batch, nb),
        in_specs=specs(fwd) + specs(bwd) + [
            pl.BlockSpec((LANES, 2 * GLA_QK_WIDTH), lambda b, i: (0, 0)),
            pl.BlockSpec((1, 2 * GLA_QK_WIDTH), lambda b, i: (0, 0)),
        ],
        out_specs=[
            pl.BlockSpec((rows, GLA_WIDTH), fwd(0)),
            pl.BlockSpec((rows, GLA_WIDTH), bwd(0)),
        ],
        out_shape=[jax.ShapeDtypeStruct((t, GLA_WIDTH), F32)] * 2,
        scratch_shapes=[pltpu.VMEM((GLA_WIDTH, GLA_QK_WIDTH), F32)] * 2,
        compiler_params=pltpu.CompilerParams(
            dimension_semantics=("arbitrary", "arbitrary"), vmem_limit_bytes=VMEM_LIMIT),
        name="gla",
    )(proj, proj, proj, lr, proj, proj, proj, lr, wg, bg)


def _out_proj_body(da_ref, of_ref, ob_ref, gr_ref, x_ref, gng_ref, wout_ref, pmg_ref, pfg_ref,
                   x1_ref, h2_ref):
    o = of_ref[...] + ob_ref[...]
    gr = gr_ref[...].astype(F32)
    parts = []
    for h in range(GLA_HEADS):
        sl = slice(h * GLA_DV, (h + 1) * GLA_DV)
        gh = gr[:, sl]
        parts.append(_rms(o[:, sl], gng_ref[...]) * (gh * jax.nn.sigmoid(gh)))
    gla = jnp.concatenate(parts, axis=1).astype(BF16)
    mix = _dot(jnp.concatenate([da_ref[...], gla], axis=1), wout_ref[...])
    x1 = x_ref[...] + _rms(mix, pmg_ref[...])
    x1_ref[...] = x1
    h2_ref[...] = _rms(x1, pfg_ref[...]).astype(BF16)


def _out_proj(da, o_f, o_b, proj, x2d, gng, w_out, pmg, pfg, tm):
    t = x2d.shape[0]
    row = lambda i: (i, 0)
    const = lambda i: (0, 0)
    return pl.pallas_call(
        _out_proj_body,
        grid=(t // tm,),
        in_specs=[
            pl.BlockSpec((tm, DA_WIDTH), row),
            pl.BlockSpec((tm, GLA_WIDTH), row),
            pl.BlockSpec((tm, GLA_WIDTH), row),
            pl.BlockSpec((tm, GLA_WIDTH), lambda i: (i, COL_GR // GLA_WIDTH)),
            pl.BlockSpec((tm, D_MODEL), row),
            pl.BlockSpec((1, GLA_DV), const),
            pl.BlockSpec((D_MODEL, D_MODEL), const),
            pl.BlockSpec((1, D_MODEL), const),
            pl.BlockSpec((1, D_MODEL), const),
        ],
        out_specs=[pl.BlockSpec((tm, D_MODEL), row), pl.BlockSpec((tm, D_MODEL), row)],
        out_shape=[jax.ShapeDtypeStruct((t, D_MODEL), F32),
                   jax.ShapeDtypeStruct((t, D_MODEL), BF16)],
        compiler_params=pltpu.CompilerParams(
            dimension_semantics=("arbitrary",), vmem_limit_bytes=VMEM_LIMIT),
        name="out_proj",
    )(da, o_f, o_b, proj, x2d, gng, w_out, pmg, pfg)


HALO = 16


def _ffn_body(h_ref, hp_ref, hn_ref, wg_ref, wu_ref, cw_ref, cb_ref, wd_ref, x1_ref, p_ref,
              pfg_ref, wpg_ref, bpg_ref, wpp_ref, png_ref, o_ref, hext_ref, acc_ref,
              *, tm, seq, nff):
    i = pl.program_id(0)
    j = pl.program_id(1)

    @pl.when(j == 0)
    def _():
        first = (i * tm) % seq == 0
        last = ((i + 1) * tm) % seq == 0
        hp = hp_ref[...]
        hn = hn_ref[...]
        hext_ref[0:HALO, :] = jnp.where(first, jnp.zeros_like(hp), hp)
        hext_ref[HALO:HALO + tm, :] = h_ref[...]
        hext_ref[HALO + tm:, :] = jnp.where(last, jnp.zeros_like(hn), hn)

    gate = _dot(hext_ref[...], wg_ref[...])
    ext = tm + 2 * HALO
    cw = cw_ref[...]
    conv = (pltpu.roll(gate, 1, 0)[HALO:HALO + tm] * cw[0:1]
            + gate[HALO:HALO + tm] * cw[1:2]
            + pltpu.roll(gate, ext - 1, 0)[HALO:HALO + tm] * cw[2:3]
            + cb_ref[...])
    up = _dot(h_ref[...], wu_ref[...])
    act = (jax.nn.gelu(conv) * up).astype(BF16)
    part = _dot(act, wd_ref[...])

    @pl.when(j == 0)
    def _():
        acc_ref[...] = part

    @pl.when(j > 0)
    def _():
        acc_ref[...] += part

    @pl.when(j == nff - 1)
    def _():
        x2 = x1_ref[...] + _rms(acc_ref[...], pfg_ref[...])
        e = _rms(_dot(p_ref[...].astype(BF16), wpp_ref[...]), png_ref[...])
        gate_e = jax.nn.sigmoid(_dot(x2.astype(BF16), wpg_ref[...]) + bpg_ref[...])
        o_ref[...] = x2 + gate_e * e


def _ffn(h2, x1, p2d, w_up, conv_w, conv_b, w_down, pfg, w_pg, b_pg, w_pp, png, seq, tm, tf):
    t = h2.shape[0]
    nff = D_FF // tf
    nhalo = t // HALO
    per = tm // HALO
    row = lambda i, j: (i, 0)
    const = lambda i, j: (0, 0)
    return pl.pallas_call(
        functools.partial(_ffn_body, tm=tm, seq=seq, nff=nff),
        grid=(t // tm, nff),
        in_specs=[
            pl.BlockSpec((tm, D_MODEL), row),
            pl.BlockSpec((HALO, D_MODEL), lambda i, j: (jnp.maximum(i * per - 1, 0), 0)),
            pl.BlockSpec((HALO, D_MODEL), lambda i, j: (jnp.minimum((i + 1) * per, nhalo - 1), 0)),
            pl.BlockSpec((D_MODEL, tf), lambda i, j: (0, j)),
            pl.BlockSpec((D_MODEL, tf), lambda i, j: (0, nff + j)),
            pl.BlockSpec((3, tf), lambda i, j: (0, j)),
            pl.BlockSpec((1, tf), lambda i, j: (0, j)),
            pl.BlockSpec((tf, D_MODEL), lambda i, j: (j, 0)),
            pl.BlockSpec((tm, D_MODEL), row),
            pl.BlockSpec((tm, PLE_DIM), row),
            pl.BlockSpec((1, D_MODEL), const),
            pl.BlockSpec((D_MODEL, D_MODEL), const),
            pl.BlockSpec((1, D_MODEL), const),
            pl.BlockSpec((PLE_DIM, D_MODEL), const),
            pl.BlockSpec((1, D_MODEL), const),
        ],
        out_specs=pl.BlockSpec((tm, D_MODEL), row),
        out_shape=jax.ShapeDtypeStruct((t, D_MODEL), F32),
        scratch_shapes=[
            pltpu.VMEM((tm + 2 * HALO, D_MODEL), BF16),
            pltpu.VMEM((tm, D_MODEL), F32),
        ],
        compiler_params=pltpu.CompilerParams(
            dimension_semantics=("arbitrary", "arbitrary"), vmem_limit_bytes=VMEM_LIMIT),
        name="conv_ffn",
    )(h2, h2, h2, w_up, w_up, conv_w, conv_b, w_down, x1, p2d, pfg, w_pg, b_pg, w_pp, png)


def _prep_weights(l, w_in, gla_w_gate_f, gla_b_gate_f, gla_w_gate_b, gla_b_gate_b, da_lq1,
                  da_lk1, da_lq2, da_lk2):
    w = w_in[l]
    qscale = jnp.concatenate([
        jnp.full((DA_WIDTH,), DA_QK_DIM ** -0.5 * LOG2E, F32), jnp.ones((2 * DA_WIDTH,), F32),
        jnp.full((GLA_QK_WIDTH,), GLA_DK ** -0.5, F32),
        jnp.ones((GLA_QK_WIDTH + 2 * GLA_WIDTH,), F32)])
    w_main = (w[:, :MAIN_WIDTH] * qscale).astype(BF16)
    w_lr = jnp.zeros((D_MODEL, LANES), F32).at[:, :2 * GLA_GATE_RANK].set(w[:, MAIN_WIDTH:])
    wg = jnp.zeros((LANES, 2 * GLA_QK_WIDTH), F32)
    wg = wg.at[:GLA_GATE_RANK, :GLA_QK_WIDTH].set(gla_w_gate_f[l])
    wg = wg.at[GLA_GATE_RANK:2 * GLA_GATE_RANK, GLA_QK_WIDTH:].set(gla_w_gate_b[l])
    bg = jnp.concatenate([gla_b_gate_f[l], gla_b_gate_b[l]])[None, :]
    lamvec = jnp.zeros((8, LANES), F32)
    for r, vec in enumerate((da_lq1, da_lk1, da_lq2, da_lk2)):
        lamvec = lamvec.at[r, :DA_QK_DIM].set(vec[l].astype(F32))
    return w_main, w_lr.astype(BF16), wg.astype(BF16), bg, lamvec


ROW_TILE = 512
ATTN_TILE = 512
GLA_ROWS = 256
FF_TILE = 512


def _layer(x2d, p2d, batch, seq, lam_init, wts):
    tm = min(seq, ROW_TILE)
    proj, q_t, v_t, lr = _in_proj(x2d, wts["pre_mix_g"], wts["w_main"], wts["w_lr"], batch, seq, tm)
    da = _diff_attention(proj, q_t, v_t, wts["lamvec"], wts["da_norm_g"], batch, seq,
                         min(seq, ATTN_TILE), lam_init)
    o_f, o_b = _gla(proj, lr, wts["wg"], wts["bg"], batch, seq, min(seq, GLA_ROWS))
    x1, h2 = _out_proj(da, o_f, o_b, proj, x2d, wts["gla_norm_g"], wts["w_out"],
                       wts["post_mix_g"], wts["pre_ffn_g"], tm)
    return _ffn(h2, x1, p2d, wts["w_ffn_up"], wts["ffn_conv_w"], wts["ffn_conv_b"],
                wts["w_ffn_down"], wts["post_ffn_g"], wts["w_ple_gate"], wts["b_ple_gate"],
                wts["w_ple_proj"], wts["ple_norm_g"], seq, tm, FF_TILE)


def kernel(x_prompt, x_sample, p_prompt, p_sample, pre_mix_g, w_in, da_lq1, da_lk1, da_lq2, da_lk2, da_norm_g, gla_w_gate_f, gla_b_gate_f, gla_w_gate_b, gla_b_gate_b, gla_norm_g, w_out, post_mix_g, pre_ffn_g, w_ffn_up, ffn_conv_w, ffn_conv_b, w_ffn_down, post_ffn_g, w_ple_gate, b_ple_gate, w_ple_proj, ple_norm_g):
    depth = w_in.shape[0]
    layers = []
    for l in range(depth):
        w_main, w_lr, wg, bg, lamvec = _prep_weights(
            l, w_in, gla_w_gate_f, gla_b_gate_f, gla_w_gate_b, gla_b_gate_b,
            da_lq1, da_lk1, da_lq2, da_lk2)
        row = lambda a: a[l][None, :].astype(F32)
        layers.append(dict(
            w_main=w_main, w_lr=w_lr, wg=wg, bg=bg, lamvec=lamvec,
            pre_mix_g=row(pre_mix_g), da_norm_g=row(da_norm_g), gla_norm_g=row(gla_norm_g),
            w_out=w_out[l].astype(BF16), post_mix_g=row(post_mix_g), pre_ffn_g=row(pre_ffn_g),
            w_ffn_up=w_ffn_up[l].astype(BF16), ffn_conv_w=ffn_conv_w[l].astype(F32),
            ffn_conv_b=row(ffn_conv_b), w_ffn_down=w_ffn_down[l].astype(BF16),
            post_ffn_g=row(post_ffn_g), w_ple_gate=w_ple_gate[l].astype(BF16),
            b_ple_gate=row(b_ple_gate), w_ple_proj=w_ple_proj[l].astype(BF16),
            ple_norm_g=row(ple_norm_g)))

    def trunk(x, p):
        batch, seq, _ = x.shape
        t = batch * seq
        x2d = x.reshape(t, D_MODEL)
        for l in range(depth):
            lam_init = 0.8 - 0.6 * math.exp(-0.3 * l)
            x2d = _layer(x2d, p[l].reshape(t, PLE_DIM), batch, seq, lam_init, layers[l])
        return x2d.reshape(batch, seq, D_MODEL)

    return (trunk(x_prompt, p_prompt), trunk(x_sample, p_sample))
```

```python
import functools
import math

import jax
import jax.numpy as jnp
from jax import lax
from jax.experimental import pallas as pl
from jax.experimental.pallas import tpu as pltpu

F32 = jnp.float32
BF16 = jnp.bfloat16

D_MODEL = 1024
PLE_DIM = 256
DA_HEADS = 4
DA_QK_DIM = 64
DA_V_DIM = 128
DA_WIDTH = DA_HEADS * DA_V_DIM
GLA_HEADS = 4
GLA_DK = 64
GLA_DV = 128
GLA_QK_WIDTH = GLA_HEADS * GLA_DK
GLA_WIDTH = GLA_HEADS * GLA_DV
GLA_GATE_RANK = 16
GLA_GATE_NORM = 16.0
GLA_CHUNK = 64
D_FF = 4 * D_MODEL
NORM_EPS = 1e-6
LANES = 128
MAIN_WIDTH = 3 * DA_WIDTH + 2 * GLA_QK_WIDTH + 2 * GLA_WIDTH
NEG_BIG = -1e30
LOG2E = math.log2(math.e)
ZERO_PROB_GAP = 150.0
ONES_ROWS = 16
VMEM_LIMIT = 56 * 1024 * 1024

COL_DA_Q = 0
COL_DA_K = DA_WIDTH
COL_DA_V = 2 * DA_WIDTH
COL_GQ = 3 * DA_WIDTH
COL_GK = COL_GQ + GLA_QK_WIDTH
COL_GV = COL_GK + GLA_QK_WIDTH
COL_GR = COL_GV + GLA_WIDTH


def _rms(x, g):
    return x * lax.rsqrt(jnp.mean(x * x, axis=-1, keepdims=True) + NORM_EPS) * g


def _nt_dot(a, b):
    return lax.dot_general(a, b, (((1,), (1,)), ((), ())), preferred_element_type=F32)


def _tn_dot(a, b):
    return lax.dot_general(a, b, (((0,), (0,)), ((), ())), preferred_element_type=F32)


def _dot(a, b):
    return jnp.dot(a, b, preferred_element_type=F32)


def _in_proj_body(x_ref, g_ref, wm_ref, wlr_ref, gsel_ref, proj_ref, qt_ref, vt_ref, lr_ref, kn_ref):
    h = _rms(x_ref[...], g_ref[...]).astype(BF16)
    main = _dot(h, wm_ref[...])
    proj_ref[...] = main.astype(BF16)
    qt_ref[...] = main[:, COL_DA_Q:COL_DA_Q + DA_WIDTH].T.astype(BF16)
    for hd in range(DA_HEADS):
        c0 = COL_DA_V + hd * DA_V_DIM
        vt_ref[hd, :DA_V_DIM, :] = main[:, c0:c0 + DA_V_DIM].T.astype(BF16)
        vt_ref[hd, DA_V_DIM:, :] = jnp.ones((ONES_ROWS, main.shape[0]), BF16)
    lr_ref[...] = _dot(h, wlr_ref[...]).astype(BF16)
    k = main[:, COL_DA_K:COL_DA_K + DA_WIDTH]
    kn_ref[...] = jnp.max(_dot((k * k).astype(BF16), gsel_ref[...]), axis=0, keepdims=True)


def _in_proj(x2d, g, w_main, w_lr, batch, seq, tm):
    t = x2d.shape[0]
    per_seq = seq // tm
    va_rows = DA_V_DIM + ONES_ROWS
    col_group = jnp.arange(DA_WIDTH, dtype=jnp.int32)[:, None] // DA_QK_DIM
    gsel = (col_group == jnp.arange(LANES, dtype=jnp.int32)[None, :]).astype(BF16)
    return pl.pallas_call(
        _in_proj_body,
        grid=(t // tm,),
        in_specs=[
            pl.BlockSpec((tm, D_MODEL), lambda i: (i, 0)),
            pl.BlockSpec((1, D_MODEL), lambda i: (0, 0)),
            pl.BlockSpec((D_MODEL, MAIN_WIDTH), lambda i: (0, 0)),
            pl.BlockSpec((D_MODEL, LANES), lambda i: (0, 0)),
            pl.BlockSpec((DA_WIDTH, LANES), lambda i: (0, 0)),
        ],
        out_specs=[
            pl.BlockSpec((tm, MAIN_WIDTH), lambda i: (i, 0)),
            pl.BlockSpec((None, DA_WIDTH, tm), lambda i: (i // per_seq, 0, i % per_seq)),
            pl.BlockSpec((None, DA_HEADS, va_rows, tm),
                         lambda i: (i // per_seq, 0, 0, i % per_seq)),
            pl.BlockSpec((tm, LANES), lambda i: (i, 0)),
            pl.BlockSpec((None, 1, LANES), lambda i: (i, 0, 0)),
        ],
        out_shape=[
            jax.ShapeDtypeStruct((t, MAIN_WIDTH), BF16),
            jax.ShapeDtypeStruct((batch, DA_WIDTH, seq), BF16),
            jax.ShapeDtypeStruct((batch, DA_HEADS, va_rows, seq), BF16),
            jax.ShapeDtypeStruct((t, LANES), BF16),
            jax.ShapeDtypeStruct((t // tm, 1, LANES), F32),
        ],
        compiler_params=pltpu.CompilerParams(
            dimension_semantics=("arbitrary",), vmem_limit_bytes=VMEM_LIMIT),
        name="in_proj",
    )(x2d, g, w_main, w_lr, gsel)


def _split3(x):
    hi = x.astype(BF16).astype(F32)
    mid = (x - hi).astype(BF16).astype(F32)
    lo = (x - hi - mid).astype(BF16).astype(F32)
    return hi, mid, lo


def _attn_consts(tq):
    i = jnp.arange(tq, dtype=jnp.int32)
    lo = (i & 255).astype(F32)
    hi = (i - (i & 255)).astype(F32)
    slopes = jnp.asarray([2.0 ** (-8.0 * (h + 1) / DA_HEADS) for h in range(DA_HEADS)], F32) * LOG2E
    pieces = jnp.stack([p for piece in _split3(slopes) for p in (piece, piece)], axis=1)
    half = jnp.zeros((DA_HEADS, DA_QK_DIM, tq), F32).at[:, :6].set(
        jnp.broadcast_to(pieces[:, :, None], (DA_HEADS, 6, tq)))
    zeros = jnp.zeros_like(half)
    left = jnp.stack([jnp.concatenate([zeros, half], 1),
                      jnp.concatenate([half, zeros], 1)], axis=1)
    augq = jnp.stack([left, -left], axis=1).astype(BF16)
    kcols = jnp.stack([lo, hi, lo, hi, lo, hi], axis=-1)
    khalf = jnp.zeros((tq, DA_QK_DIM), F32).at[:, :6].set(kcols)
    kz = jnp.zeros_like(khalf)
    augk = jnp.stack([jnp.concatenate([kz, khalf], -1),
                      jnp.concatenate([khalf, kz], -1)], axis=0).astype(BF16)
    qb = slopes[:, None] * i.astype(F32)[None, :]
    qbias = jnp.stack([-qb, qb], axis=1)[:, :, None, :]
    d = (i[None, :] - i[:, None]).astype(F32)
    corr = 2.0 * slopes[:, None, None] * jnp.minimum(d, 0.0)[None]
    return augq, augk, qbias, corr


def _attn_body(lam_ref, qt_ref, k_ref, vt_ref, kn_ref, augq_ref, augk_ref, qbias_ref, corr_ref, gn_ref,
               o_ref, qa_ref, *scratch, tq, nk, lam_init):
    grab = lambda i: (scratch[i:i + 2], scratch[i + 2:i + 4])
    s_ref, p_ref, cmax_ref, alpha_ref = grab(0), grab(4), grab(8), grab(12)
    acc_ref, m_ref = scratch[16:18], scratch[18:20]
    h = pl.program_id(1)
    qi = pl.program_id(2)
    slope = jnp.where(h == 0, 2.0 ** -2, jnp.where(h == 1, 2.0 ** -4,
                      jnp.where(h == 2, 2.0 ** -6, 2.0 ** -8))).astype(F32) * LOG2E

    qt = qt_ref[...]
    row = lax.broadcasted_iota(jnp.int32, (LANES, tq), 0)
    for side in range(2):
        qa_ref[2 * side] = jnp.where(row < DA_QK_DIM, qt, augq_ref[side, 0])
        qa_ref[2 * side + 1] = jnp.where(row >= DA_QK_DIM, qt, augq_ref[side, 1])
    for mp in range(2):
        m_ref[mp][...] = jnp.full(m_ref[mp].shape, NEG_BIG, F32)
        acc_ref[mp][...] = jnp.zeros(acc_ref[mp].shape, F32)
    lane = lax.broadcasted_iota(jnp.int32, (tq, LANES), 1)

    def scores(kt, side, slot, diag):
        k = k_ref[pl.ds(pl.multiple_of(kt * tq, tq), tq), :]
        ka = (jnp.where(lane < DA_QK_DIM, k, augk_ref[0]),
              jnp.where(lane >= DA_QK_DIM, k, augk_ref[1]))
        for mp in range(2):
            s = _dot(ka[mp], qa_ref[2 * side + mp])
            if diag:
                s = s + corr_ref[...]
            s_ref[slot][mp][...] = s
            cmax_ref[slot][mp][...] = jnp.max(s, axis=0, keepdims=True) + qbias_ref[side]

    scores(qi, 0, 0, True)

    qf = qt.astype(F32)
    lane1 = lax.broadcasted_iota(jnp.int32, (1, LANES), 1)
    kn2 = jnp.max(kn_ref[...], axis=0)
    gap = None
    for mp in range(2):
        rows = slice(mp * DA_QK_DIM, (mp + 1) * DA_QK_DIM)
        qn2 = jnp.max(jnp.sum(qf[rows] * qf[rows], axis=0, keepdims=True), axis=1, keepdims=True)
        kn2_mp = jnp.max(jnp.where(lane1 == 2 * h + mp, kn2, 0.0), axis=1, keepdims=True)
        m_min = jnp.min(cmax_ref[0][mp][...], axis=1, keepdims=True)
        g = 1.02 * jnp.sqrt(qn2 * kn2_mp) - m_min
        gap = g if gap is None else jnp.maximum(gap, g)
    far = ((gap + ZERO_PROB_GAP) / slope - 1.0) / tq
    reach = jnp.where(far < nk, jnp.maximum(jnp.ceil(far), 0.0), float(nk)).astype(jnp.int32)[0, 0]
    n_left = jnp.minimum(qi, reach)
    n_tiles = n_left + jnp.minimum(nk - 1 - qi, reach)

    def key_tile(t):
        return jnp.where(t <= n_left, qi - t, qi + (t - n_left))

    def tile_side(t):
        return (t > n_left).astype(jnp.int32)

    def tile_const(t):
        return -slope * (jnp.abs(qi - key_tile(t)) * tq).astype(F32)

    def probs(t, slot):
        shift = tile_const(t) + qbias_ref[tile_side(t)]
        for mp in range(2):
            m_old = m_ref[mp][...]
            m_new = jnp.maximum(m_old, cmax_ref[slot][mp][...] + tile_const(t))
            alpha_ref[slot][mp][...] = jnp.exp2(m_old - m_new)
            p_ref[slot][mp][...] = jnp.exp2((s_ref[slot][mp][...] - (m_new - shift)).astype(BF16))
            m_ref[mp][...] = m_new

    def accumulate(t, slot):
        va = vt_ref[:, pl.ds(pl.multiple_of(key_tile(t) * tq, tq), tq)]
        for mp in range(2):
            acc_ref[mp][...] = (alpha_ref[slot][mp][...] * acc_ref[mp][...]
                                + _dot(va, p_ref[slot][mp][...]))

    def step(t, slot):
        probs(t - 1, 1 - slot)
        scores(key_tile(t), tile_side(t), slot, False)
        accumulate(t - 1, 1 - slot)

    def pair(i, carry):
        step(2 * i + 1, 1)
        step(2 * i + 2, 0)
        return carry

    lax.fori_loop(0, n_tiles // 2, pair, 0)

    @pl.when(n_tiles % 2 == 1)
    def _():
        step(n_tiles, 1)
        probs(n_tiles, 1)
        accumulate(n_tiles, 1)

    @pl.when(n_tiles % 2 == 0)
    def _():
        probs(n_tiles, 0)
        accumulate(n_tiles, 0)

    lv = lam_ref[...]
    lam = (jnp.exp(jnp.sum(lv[0:1] * lv[1:2], axis=-1, keepdims=True))
           - jnp.exp(jnp.sum(lv[2:3] * lv[3:4], axis=-1, keepdims=True)) + lam_init)
    a0 = acc_ref[0][...]
    a1 = acc_ref[1][...]
    o0 = a0[:DA_V_DIM] * (1.0 / a0[DA_V_DIM:DA_V_DIM + 1])
    o1 = a1[:DA_V_DIM] * (1.0 / a1[DA_V_DIM:DA_V_DIM + 1])
    out = o0 - lam * o1
    inv = lax.rsqrt(jnp.mean(out * out, axis=0, keepdims=True) + NORM_EPS)
    out = out * inv * gn_ref[...] * (1.0 - lam_init)
    o_ref[...] = out.T.astype(o_ref.dtype)


def _diff_attention(proj, q_t, v_t, kn, lamvec, gn, batch, seq, tq, lam_init):
    t = proj.shape[0]
    nk = seq // tq
    augq, augk, qbias, corr = _attn_consts(tq)
    gnb = jnp.broadcast_to(gn.reshape(DA_V_DIM, 1), (DA_V_DIM, tq))
    body = functools.partial(_attn_body, tq=tq, nk=nk, lam_init=lam_init)
    kblk = COL_DA_K // LANES
    return pl.pallas_call(
        body,
        grid=(batch, DA_HEADS, nk),
        in_specs=[
            pl.BlockSpec((8, LANES), lambda b, h, i: (0, 0)),
            pl.BlockSpec((None, LANES, tq), lambda b, h, i: (b, h, i)),
            pl.BlockSpec((seq, LANES), lambda b, h, i: (b, kblk + h)),
            pl.BlockSpec((None, None, DA_V_DIM + ONES_ROWS, seq), lambda b, h, i: (b, h, 0, 0)),
            pl.BlockSpec((kn.shape[0] // batch, 1, LANES), lambda b, h, i: (b, 0, 0)),
            pl.BlockSpec((None, 2, 2, LANES, tq), lambda b, h, i: (h, 0, 0, 0, 0)),
            pl.BlockSpec((2, tq, LANES), lambda b, h, i: (0, 0, 0)),
            pl.BlockSpec((None, 2, 1, tq), lambda b, h, i: (h, 0, 0, 0)),
            pl.BlockSpec((None, tq, tq), lambda b, h, i: (h, 0, 0)),
            pl.BlockSpec((DA_V_DIM, tq), lambda b, h, i: (0, 0)),
        ],
        out_specs=pl.BlockSpec((tq, LANES), lambda b, h, i: (b * nk + i, h)),
        out_shape=jax.ShapeDtypeStruct((t, DA_WIDTH), BF16),
        scratch_shapes=[
            pltpu.VMEM((4, LANES, tq), BF16),
        ] + [pltpu.VMEM((tq, tq), F32)] * 4
        + [pltpu.VMEM((tq, tq), BF16)] * 4
        + [pltpu.VMEM((1, tq), F32)] * 4
        + [pltpu.VMEM((1, tq), F32)] * 4
        + [pltpu.VMEM((DA_V_DIM + ONES_ROWS, tq), F32)] * 2
        + [pltpu.VMEM((1, tq), F32)] * 2,
        compiler_params=pltpu.CompilerParams(
            dimension_semantics=("arbitrary", "arbitrary", "arbitrary"),
            vmem_limit_bytes=VMEM_LIMIT),
        name="diff_attention",
    )(lamvec, q_t, proj, v_t, kn, augq, augk, qbias, corr, gnb)


def _log_sigmoid(z):
    return jnp.minimum(z, 0.0) - jnp.log1p(jnp.exp(-jnp.abs(z)))


def _gla_chunk(q, k, v, g, st_ref, reverse):
    c = GLA_CHUNK
    q = q.astype(F32)
    k = k.astype(F32)
    row = lax.broadcasted_iota(jnp.int32, (c, GLA_QK_WIDTH), 0)
    lane = lax.broadcasted_iota(jnp.int32, (c, GLA_QK_WIDTH), 1)
    b = g
    shift = 1
    while shift < c:
        b = b + jnp.where(row >= shift, pltpu.roll(b, shift, 0), 0.0)
        shift *= 2
    tot = b[c - 1:c, :]
    if reverse:
        b = tot - b + g
    q_t = (q * jnp.exp(b)).astype(BF16)
    k_t = (k * jnp.exp(-b)).astype(BF16)
    k_end = (k * jnp.exp(tot - b)).astype(BF16)
    decay = jnp.exp(tot)

    head = lane // GLA_DK
    zero_k = jnp.zeros_like(k_t)
    k_bd = jnp.concatenate([jnp.where(head == h, k_t, zero_k) for h in range(GLA_HEADS)], axis=0)
    att = _nt_dot(q_t, k_bd)
    pos = lane % c
    keep = (pos > row) if reverse else (pos <= row)
    att = jnp.where(keep, att, 0.0).astype(BF16)

    vhead = lax.broadcasted_iota(jnp.int32, (c, GLA_WIDTH), 1) // GLA_DV
    zero_v = jnp.zeros_like(v)
    v_bd = jnp.concatenate([jnp.where(vhead == h, v, zero_v) for h in range(GLA_HEADS)], axis=0)
    st = st_ref[...]
    out = _dot(att, v_bd) + _nt_dot(q_t, st.astype(BF16))

    kv_t = _tn_dot(v, k_end)
    r_head = lax.broadcasted_iota(jnp.int32, kv_t.shape, 0) // GLA_DV
    c_head = lax.broadcasted_iota(jnp.int32, kv_t.shape, 1) // GLA_DK
    st_ref[...] = decay * st + jnp.where(r_head == c_head, kv_t, 0.0)
    return out


def _gla_body(qf_ref, kf_ref, vf_ref, lrf_ref, qb_ref, kb_ref, vb_ref, lrb_ref, wg_ref, bg_ref,
              of_ref, ob_ref, stf_ref, stb_ref, *, rows):
    @pl.when(pl.program_id(1) == 0)
    def _():
        stf_ref[...] = jnp.zeros(stf_ref.shape, F32)
        stb_ref[...] = jnp.zeros(stb_ref.shape, F32)

    wg = wg_ref[...]
    bg = bg_ref[...]
    zf = _dot(lrf_ref[...], wg[:, :GLA_QK_WIDTH]) + bg[:, :GLA_QK_WIDTH]
    zb = _dot(lrb_ref[...], wg[:, GLA_QK_WIDTH:]) + bg[:, GLA_QK_WIDTH:]
    gf = _log_sigmoid(zf) / GLA_GATE_NORM
    gb = _log_sigmoid(zb) / GLA_GATE_NORM
    nc = rows // GLA_CHUNK
    for ci in range(nc):
        sf = slice(ci * GLA_CHUNK, (ci + 1) * GLA_CHUNK)
        of_ref[sf, :] = _gla_chunk(qf_ref[sf, :], kf_ref[sf, :], vf_ref[sf, :], gf[sf, :],
                                   stf_ref, False)
        cj = nc - 1 - ci
        sb = slice(cj * GLA_CHUNK, (cj + 1) * GLA_CHUNK)
        ob_ref[sb, :] = _gla_chunk(qb_ref[sb, :], kb_ref[sb, :], vb_ref[sb, :], gb[sb, :],
                                   stb_ref, True)


def _gla(proj, lr, wg, bg, batch, seq, rows):
    t = proj.shape[0]
    nb = seq // rows
    qblk = COL_GQ // GLA_QK_WIDTH
    kblk = COL_GK // GLA_QK_WIDTH
    vblk = COL_GV // GLA_WIDTH

    def fwd(col):
        return lambda b, i: (b * nb + i, col)

    def bwd(col):
        return lambda b, i: (b * nb + nb - 1 - i, col)

    def specs(ix):
        return [
            pl.BlockSpec((rows, GLA_QK_WIDTH), ix(qblk)),
            pl.BlockSpec((rows, GLA_QK_WIDTH), ix(kblk)),
            pl.BlockSpec((rows, GLA_WIDTH), ix(vblk)),
            pl.BlockSpec((rows, LANES), ix(0)),
        ]

    return pl.pallas_call(
        functools.partial(_gla_body, rows=rows),
        grid=(batch, nb),
        in_specs=specs(fwd) + specs(bwd) + [
            pl.BlockSpec((LANES, 2 * GLA_QK_WIDTH), lambda b, i: (0, 0)),
            pl.BlockSpec((1, 2 * GLA_QK_WIDTH), lambda b, i: (0, 0)),
        ],
        out_specs=[
            pl.BlockSpec((rows, GLA_WIDTH), fwd(0)),
            pl.BlockSpec((rows, GLA_WIDTH), bwd(0)),
        ],
        out_shape=[jax.ShapeDtypeStruct((t, GLA_WIDTH), F32)] * 2,
        scratch_shapes=[pltpu.VMEM((GLA_WIDTH, GLA_QK_WIDTH), F32)] * 2,
        compiler_params=pltpu.CompilerParams(
            dimension_semantics=("arbitrary", "arbitrary"), vmem_limit_bytes=VMEM_LIMIT),
        name="gla",
    )(proj, proj, proj, lr, proj, proj, proj, lr, wg, bg)


def _out_proj_body(da_ref, of_ref, ob_ref, gr_ref, x_ref, gng_ref, wout_ref, pmg_ref, pfg_ref,
                   x1_ref, h2_ref):
    o = of_ref[...] + ob_ref[...]
    gr = gr_ref[...].astype(F32)
    parts = []
    for h in range(GLA_HEADS):
        sl = slice(h * GLA_DV, (h + 1) * GLA_DV)
        gh = gr[:, sl]
        parts.append(_rms(o[:, sl], gng_ref[...]) * (gh * jax.nn.sigmoid(gh)))
    gla = jnp.concatenate(parts, axis=1).astype(BF16)
    mix = _dot(jnp.concatenate([da_ref[...], gla], axis=1), wout_ref[...])
    x1 = x_ref[...] + _rms(mix, pmg_ref[...])
    x1_ref[...] = x1
    h2_ref[...] = _rms(x1, pfg_ref[...]).astype(BF16)


def _out_proj(da, o_f, o_b, proj, x2d, gng, w_out, pmg, pfg, tm):
    t = x2d.shape[0]
    row = lambda i: (i, 0)
    const = lambda i: (0, 0)
    return pl.pallas_call(
        _out_proj_body,
        grid=(t // tm,),
        in_specs=[
            pl.BlockSpec((tm, DA_WIDTH), row),
            pl.BlockSpec((tm, GLA_WIDTH), row),
            pl.BlockSpec((tm, GLA_WIDTH), row),
            pl.BlockSpec((tm, GLA_WIDTH), lambda i: (i, COL_GR // GLA_WIDTH)),
            pl.BlockSpec((tm, D_MODEL), row),
            pl.BlockSpec((1, GLA_DV), const),
            pl.BlockSpec((D_MODEL, D_MODEL), const),
            pl.BlockSpec((1, D_MODEL), const),
            pl.BlockSpec((1, D_MODEL), const),
        ],
        out_specs=[pl.BlockSpec((tm, D_MODEL), row), pl.BlockSpec((tm, D_MODEL), row)],
        out_shape=[jax.ShapeDtypeStruct((t, D_MODEL), F32),
                   jax.ShapeDtypeStruct((t, D_MODEL), BF16)],
        compiler_params=pltpu.CompilerParams(
            dimension_semantics=("arbitrary",), vmem_limit_bytes=VMEM_LIMIT),
        name="out_proj",
    )(da, o_f, o_b, proj, x2d, gng, w_out, pmg, pfg)


HALO = 16


def _ffn_body(h_ref, hp_ref, hn_ref, wg_ref, wu_ref, cw_ref, cb_ref, wd_ref, x1_ref, p_ref,
              pfg_ref, wpg_ref, bpg_ref, wpp_ref, png_ref, o_ref, hext_ref, acc_ref,
              *, tm, seq, nff):
    i = pl.program_id(0)
    j = pl.program_id(1)

    @pl.when(j == 0)
    def _():
        first = (i * tm) % seq == 0
        last = ((i + 1) * tm) % seq == 0
        hp = hp_ref[...]
        hn = hn_ref[...]
        hext_ref[0:HALO, :] = jnp.where(first, jnp.zeros_like(hp), hp)
        hext_ref[HALO:HALO + tm, :] = h_ref[...]
        hext_ref[HALO + tm:, :] = jnp.where(last, jnp.zeros_like(hn), hn)

    gate = _dot(hext_ref[...], wg_ref[...])
    ext = tm + 2 * HALO
    cw = cw_ref[...]
    conv = (pltpu.roll(gate, 1, 0)[HALO:HALO + tm] * cw[0:1]
            + gate[HALO:HALO + tm] * cw[1:2]
            + pltpu.roll(gate, ext - 1, 0)[HALO:HALO + tm] * cw[2:3]
            + cb_ref[...])
    up = _dot(h_ref[...], wu_ref[...])
    act = (jax.nn.gelu(conv) * up).astype(BF16)
    part = _dot(act, wd_ref[...])

    @pl.when(j == 0)
    def _():
        acc_ref[...] = part

    @pl.when(j > 0)
    def _():
        acc_ref[...] += part

    @pl.when(j == nff - 1)
    def _():
        x2 = x1_ref[...] + _rms(acc_ref[...], pfg_ref[...])
        e = _rms(_dot(p_ref[...].astype(BF16), wpp_ref[...]), png_ref[...])
        gate_e = jax.nn.sigmoid(_dot(x2.astype(BF16), wpg_ref[...]) + bpg_ref[...])
        o_ref[...] = x2 + gate_e * e


def _ffn(h2, x1, p2d, w_up, conv_w, conv_b, w_down, pfg, w_pg, b_pg, w_pp, png, seq, tm, tf):
    t = h2.shape[0]
    nff = D_FF // tf
    nhalo = t // HALO
    per = tm // HALO
    row = lambda i, j: (i, 0)
    const = lambda i, j: (0, 0)
    return pl.pallas_call(
        functools.partial(_ffn_body, tm=tm, seq=seq, nff=nff),
        grid=(t // tm, nff),
        in_specs=[
            pl.BlockSpec((tm, D_MODEL), row),
            pl.BlockSpec((HALO, D_MODEL), lambda i, j: (jnp.maximum(i * per - 1, 0), 0)),
            pl.BlockSpec((HALO, D_MODEL), lambda i, j: (jnp.minimum((i + 1) * per, nhalo - 1), 0)),
            pl.BlockSpec((D_MODEL, tf), lambda i, j: (0, j)),
            pl.BlockSpec((D_MODEL, tf), lambda i, j: (0, nff + j)),
            pl.BlockSpec((3, tf), lambda i, j: (0, j)),
            pl.BlockSpec((1, tf), lambda i, j: (0, j)),
            pl.BlockSpec((tf, D_MODEL), lambda i, j: (j, 0)),
            pl.BlockSpec((tm, D_MODEL), row),
            pl.BlockSpec((tm, PLE_DIM), row),
            pl.BlockSpec((1, D_MODEL), const),
            pl.BlockSpec((D_MODEL, D_MODEL), const),
            pl.BlockSpec((1, D_MODEL), const),
            pl.BlockSpec((PLE_DIM, D_MODEL), const),
            pl.BlockSpec((1, D_MODEL), const),
        ],
        out_specs=pl.BlockSpec((tm, D_MODEL), row),
        out_shape=jax.ShapeDtypeStruct((t, D_MODEL), F32),
        scratch_shapes=[
            pltpu.VMEM((tm + 2 * HALO, D_MODEL), BF16),
            pltpu.VMEM((tm, D_MODEL), F32),
        ],
        compiler_params=pltpu.CompilerParams(
            dimension_semantics=("arbitrary", "arbitrary"), vmem_limit_bytes=VMEM_LIMIT),
        name="conv_ffn",
    )(h2, h2, h2, w_up, w_up, conv_w, conv_b, w_down, x1, p2d, pfg, w_pg, b_pg, w_pp, png)


def _prep_weights(l, w_in, gla_w_gate_f, gla_b_gate_f, gla_w_gate_b, gla_b_gate_b, da_lq1,
                  da_lk1, da_lq2, da_lk2):
    w = w_in[l]
    qscale = jnp.concatenate([
        jnp.full((DA_WIDTH,), DA_QK_DIM ** -0.5 * LOG2E, F32), jnp.ones((2 * DA_WIDTH,), F32),
        jnp.full((GLA_QK_WIDTH,), GLA_DK ** -0.5, F32),
        jnp.ones((GLA_QK_WIDTH + 2 * GLA_WIDTH,), F32)])
    w_main = (w[:, :MAIN_WIDTH] * qscale).astype(BF16)
    w_lr = jnp.zeros((D_MODEL, LANES), F32).at[:, :2 * GLA_GATE_RANK].set(w[:, MAIN_WIDTH:])
    wg = jnp.zeros((LANES, 2 * GLA_QK_WIDTH), F32)
    wg = wg.at[:GLA_GATE_RANK, :GLA_QK_WIDTH].set(gla_w_gate_f[l])
    wg = wg.at[GLA_GATE_RANK:2 * GLA_GATE_RANK, GLA_QK_WIDTH:].set(gla_w_gate_b[l])
    bg = jnp.concatenate([gla_b_gate_f[l], gla_b_gate_b[l]])[None, :]
    lamvec = jnp.zeros((8, LANES), F32)
    for r, vec in enumerate((da_lq1, da_lk1, da_lq2, da_lk2)):
        lamvec = lamvec.at[r, :DA_QK_DIM].set(vec[l].astype(F32))
    return w_main, w_lr.astype(BF16), wg.astype(BF16), bg, lamvec


ROW_TILE = 512
ATTN_TILE = 512
GLA_ROWS = 256
FF_TILE = 512


def _layer(x2d, p2d, batch, seq, lam_init, wts):
    tm = min(seq, ROW_TILE)
    proj, q_t, v_t, lr, kn = _in_proj(x2d, wts["pre_mix_g"], wts["w_main"], wts["w_lr"], batch, seq, tm)
    da = _diff_attention(proj, q_t, v_t, kn, wts["lamvec"], wts["da_norm_g"], batch, seq,
                         min(seq, ATTN_TILE), lam_init)
    o_f, o_b = _gla(proj, lr, wts["wg"], wts["bg"], batch, seq, min(seq, GLA_ROWS))
    x1, h2 = _out_proj(da, o_f, o_b, proj, x2d, wts["gla_norm_g"], wts["w_out"],
                       wts["post_mix_g"], wts["pre_ffn_g"], tm)
    return _ffn(h2, x1, p2d, wts["w_ffn_up"], wts["ffn_conv_w"], wts["ffn_conv_b"],
                wts["w_ffn_down"], wts["post_ffn_g"], wts["w_ple_gate"], wts["b_ple_gate"],
                wts["w_ple_proj"], wts["ple_norm_g"], seq, tm, FF_TILE)


def kernel(x_prompt, x_sample, p_prompt, p_sample, pre_mix_g, w_in, da_lq1, da_lk1, da_lq2, da_lk2, da_norm_g, gla_w_gate_f, gla_b_gate_f, gla_w_gate_b, gla_b_gate_b, gla_norm_g, w_out, post_mix_g, pre_ffn_g, w_ffn_up, ffn_conv_w, ffn_conv_b, w_ffn_down, post_ffn_g, w_ple_gate, b_ple_gate, w_ple_proj, ple_norm_g):
    depth = w_in.shape[0]
    layers = []
    for l in range(depth):
        w_main, w_lr, wg, bg, lamvec = _prep_weights(
            l, w_in, gla_w_gate_f, gla_b_gate_f, gla_w_gate_b, gla_b_gate_b,
            da_lq1, da_lk1, da_lq2, da_lk2)
        row = lambda a: a[l][None, :].astype(F32)
        layers.append(dict(
            w_main=w_main, w_lr=w_lr, wg=wg, bg=bg, lamvec=lamvec,
            pre_mix_g=row(pre_mix_g), da_norm_g=row(da_norm_g), gla_norm_g=row(gla_norm_g),
            w_out=w_out[l].astype(BF16), post_mix_g=row(post_mix_g), pre_ffn_g=row(pre_ffn_g),
            w_ffn_up=w_ffn_up[l].astype(BF16), ffn_conv_w=ffn_conv_w[l].astype(F32),
            ffn_conv_b=row(ffn_conv_b), w_ffn_down=w_ffn_down[l].astype(BF16),
            post_ffn_g=row(post_ffn_g), w_ple_gate=w_ple_gate[l].astype(BF16),
            b_ple_gate=row(b_ple_gate), w_ple_proj=w_ple_proj[l].astype(BF16),
            ple_norm_g=row(ple_norm_g)))

    def trunk(x, p):
        batch, seq, _ = x.shape
        t = batch * seq
        x2d = x.reshape(t, D_MODEL)
        for l in range(depth):
            lam_init = 0.8 - 0.6 * math.exp(-0.3 * l)
            x2d = _layer(x2d, p[l].reshape(t, PLE_DIM), batch, seq, lam_init, layers[l])
        return x2d.reshape(batch, seq, D_MODEL)

    return (trunk(x_prompt, p_prompt), trunk(x_sample, p_sample))
```

```python
import functools
import math

import jax
import jax.numpy as jnp
from jax import lax
from jax.experimental import pallas as pl
from jax.experimental.pallas import tpu as pltpu

F32 = jnp.float32
BF16 = jnp.bfloat16

D_MODEL = 1024
PLE_DIM = 256
DA_HEADS = 4
DA_QK_DIM = 64
DA_V_DIM = 128
DA_WIDTH = DA_HEADS * DA_V_DIM
GLA_HEADS = 4
GLA_DK = 64
GLA_DV = 128
GLA_QK_WIDTH = GLA_HEADS * GLA_DK
GLA_WIDTH = GLA_HEADS * GLA_DV
GLA_GATE_RANK = 16
GLA_GATE_NORM = 16.0
GLA_CHUNK = 64
D_FF = 4 * D_MODEL
NORM_EPS = 1e-6
LANES = 128
MAIN_WIDTH = 3 * DA_WIDTH + 2 * GLA_QK_WIDTH + 2 * GLA_WIDTH
NEG_BIG = -1e30
LOG2E = math.log2(math.e)
ZERO_PROB_GAP = 150.0
ONES_ROWS = 16
VMEM_LIMIT = 56 * 1024 * 1024

COL_DA_Q = 0
COL_DA_K = DA_WIDTH
COL_DA_V = 2 * DA_WIDTH
COL_GQ = 3 * DA_WIDTH
COL_GK = COL_GQ + GLA_QK_WIDTH
COL_GV = COL_GK + GLA_QK_WIDTH
COL_GR = COL_GV + GLA_WIDTH


def _rms(x, g):
    return x * lax.rsqrt(jnp.mean(x * x, axis=-1, keepdims=True) + NORM_EPS) * g


def _nt_dot(a, b):
    return lax.dot_general(a, b, (((1,), (1,)), ((), ())), preferred_element_type=F32)


def _tn_dot(a, b):
    return lax.dot_general(a, b, (((0,), (0,)), ((), ())), preferred_element_type=F32)


def _dot(a, b):
    return jnp.dot(a, b, preferred_element_type=F32)


def _in_proj_body(x_ref, g_ref, wm_ref, wlr_ref, gsel_ref, proj_ref, qt_ref, vt_ref, lr_ref, kn_ref):
    h = _rms(x_ref[...], g_ref[...]).astype(BF16)
    main = _dot(h, wm_ref[...])
    proj_ref[...] = main.astype(BF16)
    qt_ref[...] = main[:, COL_DA_Q:COL_DA_Q + DA_WIDTH].T.astype(BF16)
    for hd in range(DA_HEADS):
        c0 = COL_DA_V + hd * DA_V_DIM
        vt_ref[hd, :DA_V_DIM, :] = main[:, c0:c0 + DA_V_DIM].T.astype(BF16)
        vt_ref[hd, DA_V_DIM:, :] = jnp.ones((ONES_ROWS, main.shape[0]), BF16)
    lr_ref[...] = _dot(h, wlr_ref[...]).astype(BF16)
    k = main[:, COL_DA_K:COL_DA_K + DA_WIDTH]
    kn_ref[...] = jnp.max(_dot((k * k).astype(BF16), gsel_ref[...]), axis=0, keepdims=True)


def _in_proj(x2d, g, w_main, w_lr, batch, seq, tm):
    t = x2d.shape[0]
    per_seq = seq // tm
    va_rows = DA_V_DIM + ONES_ROWS
    col_group = jnp.arange(DA_WIDTH, dtype=jnp.int32)[:, None] // DA_QK_DIM
    gsel = (col_group == jnp.arange(LANES, dtype=jnp.int32)[None, :]).astype(BF16)
    return pl.pallas_call(
        _in_proj_body,
        grid=(t // tm,),
        in_specs=[
            pl.BlockSpec((tm, D_MODEL), lambda i: (i, 0)),
            pl.BlockSpec((1, D_MODEL), lambda i: (0, 0)),
            pl.BlockSpec((D_MODEL, MAIN_WIDTH), lambda i: (0, 0)),
            pl.BlockSpec((D_MODEL, LANES), lambda i: (0, 0)),
            pl.BlockSpec((DA_WIDTH, LANES), lambda i: (0, 0)),
        ],
        out_specs=[
            pl.BlockSpec((tm, MAIN_WIDTH), lambda i: (i, 0)),
            pl.BlockSpec((None, DA_WIDTH, tm), lambda i: (i // per_seq, 0, i % per_seq)),
            pl.BlockSpec((None, DA_HEADS, va_rows, tm),
                         lambda i: (i // per_seq, 0, 0, i % per_seq)),
            pl.BlockSpec((tm, LANES), lambda i: (i, 0)),
            pl.BlockSpec((None, 1, LANES), lambda i: (i, 0, 0)),
        ],
        out_shape=[
            jax.ShapeDtypeStruct((t, MAIN_WIDTH), BF16),
            jax.ShapeDtypeStruct((batch, DA_WIDTH, seq), BF16),
            jax.ShapeDtypeStruct((batch, DA_HEADS, va_rows, seq), BF16),
            jax.ShapeDtypeStruct((t, LANES), BF16),
            jax.ShapeDtypeStruct((t // tm, 1, LANES), F32),
        ],
        compiler_params=pltpu.CompilerParams(
            dimension_semantics=("arbitrary",), vmem_limit_bytes=VMEM_LIMIT),
        name="in_proj",
    )(x2d, g, w_main, w_lr, gsel)


def _split3(x):
    hi = x.astype(BF16).astype(F32)
    mid = (x - hi).astype(BF16).astype(F32)
    lo = (x - hi - mid).astype(BF16).astype(F32)
    return hi, mid, lo


def _attn_consts(tq):
    i = jnp.arange(tq, dtype=jnp.int32)
    lo = (i & 255).astype(F32)
    hi = (i - (i & 255)).astype(F32)
    slopes = jnp.asarray([2.0 ** (-8.0 * (h + 1) / DA_HEADS) for h in range(DA_HEADS)], F32) * LOG2E
    pieces = jnp.stack([p for piece in _split3(slopes) for p in (piece, piece)], axis=1)
    half = jnp.zeros((DA_HEADS, DA_QK_DIM, tq), F32).at[:, :6].set(
        jnp.broadcast_to(pieces[:, :, None], (DA_HEADS, 6, tq)))
    zeros = jnp.zeros_like(half)
    left = jnp.stack([jnp.concatenate([zeros, half], 1),
                      jnp.concatenate([half, zeros], 1)], axis=1)
    augq = jnp.stack([left, -left], axis=1).astype(BF16)
    kcols = jnp.stack([lo, hi, lo, hi, lo, hi], axis=-1)
    khalf = jnp.zeros((tq, DA_QK_DIM), F32).at[:, :6].set(kcols)
    kz = jnp.zeros_like(khalf)
    augk = jnp.stack([jnp.concatenate([kz, khalf], -1),
                      jnp.concatenate([khalf, kz], -1)], axis=0).astype(BF16)
    qb = slopes[:, None] * i.astype(F32)[None, :]
    qbias = jnp.stack([-qb, qb], axis=1)[:, :, None, :]
    d = (i[None, :] - i[:, None]).astype(F32)
    corr = 2.0 * slopes[:, None, None] * jnp.minimum(d, 0.0)[None]
    return augq, augk, qbias, corr


def _attn_body(lam_ref, qt_ref, k_ref, vt_ref, kn_ref, augq_ref, augk_ref, qbias_ref, corr_ref, gn_ref,
               o_ref, qa_ref, *scratch, tq, nk, lam_init):
    grab = lambda i: (scratch[i:i + 2], scratch[i + 2:i + 4])
    s_ref, p_ref, cmax_ref, alpha_ref = grab(0), grab(4), grab(8), grab(12)
    acc_ref, m_ref = scratch[16:18], scratch[18:20]
    h = pl.program_id(1)
    qi = pl.program_id(2)
    slope = jnp.where(h == 0, 2.0 ** -2, jnp.where(h == 1, 2.0 ** -4,
                      jnp.where(h == 2, 2.0 ** -6, 2.0 ** -8))).astype(F32) * LOG2E

    qt = qt_ref[...]
    row = lax.broadcasted_iota(jnp.int32, (LANES, tq), 0)
    for side in range(2):
        qa_ref[2 * side] = jnp.where(row < DA_QK_DIM, qt, augq_ref[side, 0])
        qa_ref[2 * side + 1] = jnp.where(row >= DA_QK_DIM, qt, augq_ref[side, 1])
    for mp in range(2):
        m_ref[mp][...] = jnp.full(m_ref[mp].shape, NEG_BIG, F32)
        acc_ref[mp][...] = jnp.zeros(acc_ref[mp].shape, F32)
    lane = lax.broadcasted_iota(jnp.int32, (tq, LANES), 1)

    def scores(kt, side, slot, diag):
        k = k_ref[pl.ds(pl.multiple_of(kt * tq, tq), tq), :]
        ka = (jnp.where(lane < DA_QK_DIM, k, augk_ref[0]),
              jnp.where(lane >= DA_QK_DIM, k, augk_ref[1]))
        for mp in range(2):
            s = _dot(ka[mp], qa_ref[2 * side + mp])
            if diag:
                s = s + corr_ref[...]
            s_ref[slot][mp][...] = s
            cmax_ref[slot][mp][...] = jnp.max(s, axis=0, keepdims=True) + qbias_ref[side]

    scores(qi, 0, 0, True)

    qf = qt.astype(F32)
    lane1 = lax.broadcasted_iota(jnp.int32, (1, LANES), 1)
    kn2 = jnp.max(kn_ref[...], axis=0)
    gap = None
    for mp in range(2):
        rows = slice(mp * DA_QK_DIM, (mp + 1) * DA_QK_DIM)
        qn2 = jnp.max(jnp.sum(qf[rows] * qf[rows], axis=0, keepdims=True), axis=1, keepdims=True)
        kn2_mp = jnp.max(jnp.where(lane1 == 2 * h + mp, kn2, 0.0), axis=1, keepdims=True)
        m_min = jnp.min(cmax_ref[0][mp][...], axis=1, keepdims=True)
        g = 1.02 * jnp.sqrt(qn2 * kn2_mp) - m_min
        gap = g if gap is None else jnp.maximum(gap, g)
    far = ((gap + ZERO_PROB_GAP) / slope - 1.0) / tq
    reach = jnp.where(far < nk, jnp.maximum(jnp.ceil(far), 0.0), float(nk)).astype(jnp.int32)[0, 0]
    n_left = jnp.minimum(qi, reach)
    n_tiles = n_left + jnp.minimum(nk - 1 - qi, reach)

    def key_tile(t):
        return jnp.where(t <= n_left, qi - t, qi + (t - n_left))

    def tile_side(t):
        return (t > n_left).astype(jnp.int32)

    def tile_const(t):
        return -slope * (jnp.abs(qi - key_tile(t)) * tq).astype(F32)

    def probs(t, slot):
        shift = tile_const(t) + qbias_ref[tile_side(t)]
        for mp in range(2):
            m_old = m_ref[mp][...]
            m_new = jnp.maximum(m_old, cmax_ref[slot][mp][...] + tile_const(t))
            alpha_ref[slot][mp][...] = jnp.exp2(m_old - m_new)
            p_ref[slot][mp][...] = jnp.exp2((s_ref[slot][mp][...] - (m_new - shift)).astype(BF16))
            m_ref[mp][...] = m_new

    def accumulate(t, slot):
        va = vt_ref[:, pl.ds(pl.multiple_of(key_tile(t) * tq, tq), tq)]
        for mp in range(2):
            acc_ref[mp][...] = (alpha_ref[slot][mp][...] * acc_ref[mp][...]
                                + _dot(va, p_ref[slot][mp][...]))

    def step(t, slot):
        probs(t - 1, 1 - slot)
        scores(key_tile(t), tile_side(t), slot, False)
        accumulate(t - 1, 1 - slot)

    def pair(i, carry):
        step(2 * i + 1, 1)
        step(2 * i + 2, 0)
        return carry

    lax.fori_loop(0, n_tiles // 2, pair, 0)

    @pl.when(n_tiles % 2 == 1)
    def _():
        step(n_tiles, 1)
        probs(n_tiles, 1)
        accumulate(n_tiles, 1)

    @pl.when(n_tiles % 2 == 0)
    def _():
        probs(n_tiles, 0)
        accumulate(n_tiles, 0)

    lv = lam_ref[...]
    lam = (jnp.exp(jnp.sum(lv[0:1] * lv[1:2], axis=-1, keepdims=True))
           - jnp.exp(jnp.sum(lv[2:3] * lv[3:4], axis=-1, keepdims=True)) + lam_init)
    a0 = acc_ref[0][...]
    a1 = acc_ref[1][...]
    o0 = a0[:DA_V_DIM] * (1.0 / a0[DA_V_DIM:DA_V_DIM + 1])
    o1 = a1[:DA_V_DIM] * (1.0 / a1[DA_V_DIM:DA_V_DIM + 1])
    out = o0 - lam * o1
    inv = lax.rsqrt(jnp.mean(out * out, axis=0, keepdims=True) + NORM_EPS)
    out = out * inv * gn_ref[...] * (1.0 - lam_init)
    o_ref[...] = out.T.astype(o_ref.dtype)


def _diff_attention(proj, q_t, v_t, kn, lamvec, gn, batch, seq, tq, lam_init):
    t = proj.shape[0]
    nk = seq // tq
    augq, augk, qbias, corr = _attn_consts(tq)
    gnb = jnp.broadcast_to(gn.reshape(DA_V_DIM, 1), (DA_V_DIM, tq))
    body = functools.partial(_attn_body, tq=tq, nk=nk, lam_init=lam_init)
    kblk = COL_DA_K // LANES
    return pl.pallas_call(
        body,
        grid=(batch, DA_HEADS, nk),
        in_specs=[
            pl.BlockSpec((8, LANES), lambda b, h, i: (0, 0)),
            pl.BlockSpec((None, LANES, tq), lambda b, h, i: (b, h, i)),
            pl.BlockSpec((seq, LANES), lambda b, h, i: (b, kblk + h)),
            pl.BlockSpec((None, None, DA_V_DIM + ONES_ROWS, seq), lambda b, h, i: (b, h, 0, 0)),
            pl.BlockSpec((kn.shape[0] // batch, 1, LANES), lambda b, h, i: (b, 0, 0)),
            pl.BlockSpec((None, 2, 2, LANES, tq), lambda b, h, i: (h, 0, 0, 0, 0)),
            pl.BlockSpec((2, tq, LANES), lambda b, h, i: (0, 0, 0)),
            pl.BlockSpec((None, 2, 1, tq), lambda b, h, i: (h, 0, 0, 0)),
            pl.BlockSpec((None, tq, tq), lambda b, h, i: (h, 0, 0)),
            pl.BlockSpec((DA_V_DIM, tq), lambda b, h, i: (0, 0)),
        ],
        out_specs=pl.BlockSpec((tq, LANES), lambda b, h, i: (b * nk + i, h)),
        out_shape=jax.ShapeDtypeStruct((t, DA_WIDTH), BF16),
        scratch_shapes=[
            pltpu.VMEM((4, LANES, tq), BF16),
        ] + [pltpu.VMEM((tq, tq), F32)] * 4
        + [pltpu.VMEM((tq, tq), BF16)] * 4
        + [pltpu.VMEM((1, tq), F32)] * 4
        + [pltpu.VMEM((1, tq), F32)] * 4
        + [pltpu.VMEM((DA_V_DIM + ONES_ROWS, tq), F32)] * 2
        + [pltpu.VMEM((1, tq), F32)] * 2,
        compiler_params=pltpu.CompilerParams(
            dimension_semantics=("arbitrary", "arbitrary", "arbitrary"),
            vmem_limit_bytes=VMEM_LIMIT),
        name="diff_attention",
    )(lamvec, q_t, proj, v_t, kn, augq, augk, qbias, corr, gnb)


def _log_sigmoid(z):
    return jnp.minimum(z, 0.0) - jnp.log1p(jnp.exp(-jnp.abs(z)))


def _gla_chunk(q, k, v, g, st_ref, reverse):
    c = GLA_CHUNK
    q = q.astype(F32)
    k = k.astype(F32)
    row = lax.broadcasted_iota(jnp.int32, (c, GLA_QK_WIDTH), 0)
    lane = lax.broadcasted_iota(jnp.int32, (c, GLA_QK_WIDTH), 1)
    b = g
    shift = 1
    while shift < c:
        b = b + jnp.where(row >= shift, pltpu.roll(b, shift, 0), 0.0)
        shift *= 2
    tot = b[c - 1:c, :]
    if reverse:
        b = tot - b + g
    q_t = (q * jnp.exp(b)).astype(BF16)
    k_t = (k * jnp.exp(-b)).astype(BF16)
    k_end = (k * jnp.exp(tot - b)).astype(BF16)
    decay = jnp.exp(tot)

    head = lane // GLA_DK
    zero_k = jnp.zeros_like(k_t)
    k_bd = jnp.concatenate([jnp.where(head == h, k_t, zero_k) for h in range(GLA_HEADS)], axis=0)
    att = _nt_dot(q_t, k_bd)
    pos = lane % c
    keep = (pos > row) if reverse else (pos <= row)
    att = jnp.where(keep, att, 0.0).astype(BF16)

    vhead = lax.broadcasted_iota(jnp.int32, (c, GLA_WIDTH), 1) // GLA_DV
    zero_v = jnp.zeros_like(v)
    v_bd = jnp.concatenate([jnp.where(vhead == h, v, zero_v) for h in range(GLA_HEADS)], axis=0)
    st = st_ref[...]
    out = _dot(att, v_bd) + _nt_dot(q_t, st.astype(BF16))

    kv_t = _tn_dot(v, k_end)
    r_head = lax.broadcasted_iota(jnp.int32, kv_t.shape, 0) // GLA_DV
    c_head = lax.broadcasted_iota(jnp.int32, kv_t.shape, 1) // GLA_DK
    st_ref[...] = decay * st + jnp.where(r_head == c_head, kv_t, 0.0)
    return out


def _gla_body(qf_ref, kf_ref, vf_ref, lrf_ref, qb_ref, kb_ref, vb_ref, lrb_ref, wg_ref, bg_ref,
              of_ref, ob_ref, stf_ref, stb_ref, *, rows):
    @pl.when(pl.program_id(1) == 0)
    def _():
        stf_ref[...] = jnp.zeros(stf_ref.shape, F32)
        stb_ref[...] = jnp.zeros(stb_ref.shape, F32)

    wg = wg_ref[...]
    bg = bg_ref[...]
    zf = _dot(lrf_ref[...], wg[:, :GLA_QK_WIDTH]) + bg[:, :GLA_QK_WIDTH]
    zb = _dot(lrb_ref[...], wg[:, GLA_QK_WIDTH:]) + bg[:, GLA_QK_WIDTH:]
    gf = _log_sigmoid(zf) / GLA_GATE_NORM
    gb = _log_sigmoid(zb) / GLA_GATE_NORM
    nc = rows // GLA_CHUNK
    for ci in range(nc):
        sf = slice(ci * GLA_CHUNK, (ci + 1) * GLA_CHUNK)
        of_ref[sf, :] = _gla_chunk(qf_ref[sf, :], kf_ref[sf, :], vf_ref[sf, :], gf[sf, :],
                                   stf_ref, False)
        cj = nc - 1 - ci
        sb = slice(cj * GLA_CHUNK, (cj + 1) * GLA_CHUNK)
        ob_ref[sb, :] = _gla_chunk(qb_ref[sb, :], kb_ref[sb, :], vb_ref[sb, :], gb[sb, :],
                                   stb_ref, True)


def _gla(proj, lr, wg, bg, batch, seq, rows):
    t = proj.shape[0]
    nb = seq // rows
    qblk = COL_GQ // GLA_QK_WIDTH
    kblk = COL_GK // GLA_QK_WIDTH
    vblk = COL_GV // GLA_WIDTH

    def fwd(col):
        return lambda b, i: (b * nb + i, col)

    def bwd(col):
        return lambda b, i: (b * nb + nb - 1 - i, col)

    def specs(ix):
        return [
            pl.BlockSpec((rows, GLA_QK_WIDTH), ix(qblk)),
            pl.BlockSpec((rows, GLA_QK_WIDTH), ix(kblk)),
            pl.BlockSpec((rows, GLA_WIDTH), ix(vblk)),
            pl.BlockSpec((rows, LANES), ix(0)),
        ]

    return pl.pallas_call(
        functools.partial(_gla_body, rows=rows),
        grid=(batch, nb),
        in_specs=specs(fwd) + specs(bwd) + [
            pl.BlockSpec((LANES, 2 * GLA_QK_WIDTH), lambda b, i: (0, 0)),
            pl.BlockSpec((1, 2 * GLA_QK_WIDTH), lambda b, i: (0, 0)),
        ],
        out_specs=[
            pl.BlockSpec((rows, GLA_WIDTH), fwd(0)),
            pl.BlockSpec((rows, GLA_WIDTH), bwd(0)),
        ],
        out_shape=[jax.ShapeDtypeStruct((t, GLA_WIDTH), F32)] * 2,
        scratch_shapes=[pltpu.VMEM((GLA_WIDTH, GLA_QK_WIDTH), F32)] * 2,
        compiler_params=pltpu.CompilerParams(
            dimension_semantics=("arbitrary", "arbitrary"), vmem_limit_bytes=VMEM_LIMIT),
        name="gla",
    )(proj, proj, proj, lr, proj, proj, proj, lr, wg, bg)


def _out_proj_body(da_ref, of_ref, ob_ref, gr_ref, x_ref, gng_ref, wout_ref, pmg_ref, pfg_ref,
                   x1_ref, h2_ref):
    o = of_ref[...] + ob_ref[...]
    gr = gr_ref[...].astype(F32)
    parts = []
    for h in range(GLA_HEADS):
        sl = slice(h * GLA_DV, (h + 1) * GLA_DV)
        gh = gr[:, sl]
        parts.append(_rms(o[:, sl], gng_ref[...]) * (gh * jax.nn.sigmoid(gh)))
    gla = jnp.concatenate(parts, axis=1).astype(BF16)
    mix = _dot(jnp.concatenate([da_ref[...], gla], axis=1), wout_ref[...])
    x1 = x_ref[...] + _rms(mix, pmg_ref[...])
    x1_ref[...] = x1
    h2_ref[...] = _rms(x1, pfg_ref[...]).astype(BF16)


def _out_proj(da, o_f, o_b, proj, x2d, gng, w_out, pmg, pfg, tm):
    t = x2d.shape[0]
    row = lambda i: (i, 0)
    const = lambda i: (0, 0)
    return pl.pallas_call(
        _out_proj_body,
        grid=(t // tm,),
        in_specs=[
            pl.BlockSpec((tm, DA_WIDTH), row),
            pl.BlockSpec((tm, GLA_WIDTH), row),
            pl.BlockSpec((tm, GLA_WIDTH), row),
            pl.BlockSpec((tm, GLA_WIDTH), lambda i: (i, COL_GR // GLA_WIDTH)),
            pl.BlockSpec((tm, D_MODEL), row),
            pl.BlockSpec((1, GLA_DV), const),
            pl.BlockSpec((D_MODEL, D_MODEL), const),
            pl.BlockSpec((1, D_MODEL), const),
            pl.BlockSpec((1, D_MODEL), const),
        ],
        out_specs=[pl.BlockSpec((tm, D_MODEL), row), pl.BlockSpec((tm, D_MODEL), row)],
        out_shape=[jax.ShapeDtypeStruct((t, D_MODEL), F32),
                   jax.ShapeDtypeStruct((t, D_MODEL), BF16)],
        compiler_params=pltpu.CompilerParams(
            dimension_semantics=("arbitrary",), vmem_limit_bytes=VMEM_LIMIT),
        name="out_proj",
    )(da, o_f, o_b, proj, x2d, gng, w_out, pmg, pfg)


HALO = 16


def _ffn_body(h_ref, hp_ref, hn_ref, wg_ref, wu_ref, cw_ref, cb_ref, wd_ref, x1_ref, p_ref,
              pfg_ref, wpg_ref, bpg_ref, wpp_ref, png_ref, o_ref, hext_ref, act_ref,
              *, tm, tf, seq, nff):
    i = pl.program_id(0)
    j = pl.program_id(1)

    @pl.when(j == 0)
    def _():
        first = (i * tm) % seq == 0
        last = ((i + 1) * tm) % seq == 0
        hp = hp_ref[...]
        hn = hn_ref[...]
        hext_ref[0:HALO, :] = jnp.where(first, jnp.zeros_like(hp), hp)
        hext_ref[HALO:HALO + tm, :] = h_ref[...]
        hext_ref[HALO + tm:, :] = jnp.where(last, jnp.zeros_like(hn), hn)

    gate = _dot(hext_ref[...], wg_ref[...])
    ext = tm + 2 * HALO
    cw = cw_ref[...]
    conv = (pltpu.roll(gate, 1, 0)[HALO:HALO + tm] * cw[0:1]
            + gate[HALO:HALO + tm] * cw[1:2]
            + pltpu.roll(gate, ext - 1, 0)[HALO:HALO + tm] * cw[2:3]
            + cb_ref[...])
    up = _dot(h_ref[...], wu_ref[...])
    act_ref[:, pl.ds(pl.multiple_of(j * tf, tf), tf)] = (jax.nn.gelu(conv) * up).astype(BF16)

    @pl.when(j == nff - 1)
    def _():
        x2 = x1_ref[...] + _rms(_dot(act_ref[...], wd_ref[...]), pfg_ref[...])
        e = _rms(_dot(p_ref[...].astype(BF16), wpp_ref[...]), png_ref[...])
        gate_e = jax.nn.sigmoid(_dot(x2.astype(BF16), wpg_ref[...]) + bpg_ref[...])
        o_ref[...] = x2 + gate_e * e


def _ffn(h2, x1, p2d, w_up, conv_w, conv_b, w_down, pfg, w_pg, b_pg, w_pp, png, seq, tm, tf):
    t = h2.shape[0]
    nff = D_FF // tf
    nhalo = t // HALO
    per = tm // HALO
    row = lambda i, j: (i, 0)
    const = lambda i, j: (0, 0)
    return pl.pallas_call(
        functools.partial(_ffn_body, tm=tm, tf=tf, seq=seq, nff=nff),
        grid=(t // tm, nff),
        in_specs=[
            pl.BlockSpec((tm, D_MODEL), row),
            pl.BlockSpec((HALO, D_MODEL), lambda i, j: (jnp.maximum(i * per - 1, 0), 0)),
            pl.BlockSpec((HALO, D_MODEL), lambda i, j: (jnp.minimum((i + 1) * per, nhalo - 1), 0)),
            pl.BlockSpec((D_MODEL, tf), lambda i, j: (0, j)),
            pl.BlockSpec((D_MODEL, tf), lambda i, j: (0, nff + j)),
            pl.BlockSpec((3, tf), lambda i, j: (0, j)),
            pl.BlockSpec((1, tf), lambda i, j: (0, j)),
            pl.BlockSpec((D_FF, D_MODEL), const, pipeline_mode=pl.Buffered(1)),
            pl.BlockSpec((tm, D_MODEL), row),
            pl.BlockSpec((tm, PLE_DIM), row),
            pl.BlockSpec((1, D_MODEL), const),
            pl.BlockSpec((D_MODEL, D_MODEL), const, pipeline_mode=pl.Buffered(1)),
            pl.BlockSpec((1, D_MODEL), const),
            pl.BlockSpec((PLE_DIM, D_MODEL), const, pipeline_mode=pl.Buffered(1)),
            pl.BlockSpec((1, D_MODEL), const),
        ],
        out_specs=pl.BlockSpec((tm, D_MODEL), row),
        out_shape=jax.ShapeDtypeStruct((t, D_MODEL), F32),
        scratch_shapes=[
            pltpu.VMEM((tm + 2 * HALO, D_MODEL), BF16),
            pltpu.VMEM((tm, D_FF), BF16),
        ],
        compiler_params=pltpu.CompilerParams(
            dimension_semantics=("arbitrary", "arbitrary"), vmem_limit_bytes=VMEM_LIMIT),
        name="conv_ffn",
    )(h2, h2, h2, w_up, w_up, conv_w, conv_b, w_down, x1, p2d, pfg, w_pg, b_pg, w_pp, png)


def _prep_weights(l, w_in, gla_w_gate_f, gla_b_gate_f, gla_w_gate_b, gla_b_gate_b, da_lq1,
                  da_lk1, da_lq2, da_lk2):
    w = w_in[l]
    qscale = jnp.concatenate([
        jnp.full((DA_WIDTH,), DA_QK_DIM ** -0.5 * LOG2E, F32), jnp.ones((2 * DA_WIDTH,), F32),
        jnp.full((GLA_QK_WIDTH,), GLA_DK ** -0.5, F32),
        jnp.ones((GLA_QK_WIDTH + 2 * GLA_WIDTH,), F32)])
    w_main = (w[:, :MAIN_WIDTH] * qscale).astype(BF16)
    w_lr = jnp.zeros((D_MODEL, LANES), F32).at[:, :2 * GLA_GATE_RANK].set(w[:, MAIN_WIDTH:])
    wg = jnp.zeros((LANES, 2 * GLA_QK_WIDTH), F32)
    wg = wg.at[:GLA_GATE_RANK, :GLA_QK_WIDTH].set(gla_w_gate_f[l])
    wg = wg.at[GLA_GATE_RANK:2 * GLA_GATE_RANK, GLA_QK_WIDTH:].set(gla_w_gate_b[l])
    bg = jnp.concatenate([gla_b_gate_f[l], gla_b_gate_b[l]])[None, :]
    lamvec = jnp.zeros((8, LANES), F32)
    for r, vec in enumerate((da_lq1, da_lk1, da_lq2, da_lk2)):
        lamvec = lamvec.at[r, :DA_QK_DIM].set(vec[l].astype(F32))
    return w_main, w_lr.astype(BF16), wg.astype(BF16), bg, lamvec


ROW_TILE = 512
ATTN_TILE = 1024
GLA_ROWS = 256
FF_TILE = 512


def _attn_tile(seq):
    tile = min(seq, ATTN_TILE)
    while tile > LANES:
        resident = 2 * 2 * seq * (LANES + DA_V_DIM + ONES_ROWS)
        tiles = tile * tile * (4 * 4 + 4 * 2 + 2 * 4)
        if resident + tiles <= VMEM_LIMIT * 3 // 4:
            break
        tile //= 2
    return tile


def _layer(x2d, p2d, batch, seq, lam_init, wts):
    tm = min(seq, ROW_TILE)
    proj, q_t, v_t, lr, kn = _in_proj(x2d, wts["pre_mix_g"], wts["w_main"], wts["w_lr"], batch, seq, tm)
    da = _diff_attention(proj, q_t, v_t, kn, wts["lamvec"], wts["da_norm_g"], batch, seq,
                         _attn_tile(seq), lam_init)
    o_f, o_b = _gla(proj, lr, wts["wg"], wts["bg"], batch, seq, min(seq, GLA_ROWS))
    x1, h2 = _out_proj(da, o_f, o_b, proj, x2d, wts["gla_norm_g"], wts["w_out"],
                       wts["post_mix_g"], wts["pre_ffn_g"], tm)
    return _ffn(h2, x1, p2d, wts["w_ffn_up"], wts["ffn_conv_w"], wts["ffn_conv_b"],
                wts["w_ffn_down"], wts["post_ffn_g"], wts["w_ple_gate"], wts["b_ple_gate"],
                wts["w_ple_proj"], wts["ple_norm_g"], seq, tm, FF_TILE)


def kernel(x_prompt, x_sample, p_prompt, p_sample, pre_mix_g, w_in, da_lq1, da_lk1, da_lq2, da_lk2, da_norm_g, gla_w_gate_f, gla_b_gate_f, gla_w_gate_b, gla_b_gate_b, gla_norm_g, w_out, post_mix_g, pre_ffn_g, w_ffn_up, ffn_conv_w, ffn_conv_b, w_ffn_down, post_ffn_g, w_ple_gate, b_ple_gate, w_ple_proj, ple_norm_g):
    depth = w_in.shape[0]
    layers = []
    for l in range(depth):
        w_main, w_lr, wg, bg, lamvec = _prep_weights(
            l, w_in, gla_w_gate_f, gla_b_gate_f, gla_w_gate_b, gla_b_gate_b,
            da_lq1, da_lk1, da_lq2, da_lk2)
        row = lambda a: a[l][None, :].astype(F32)
        layers.append(dict(
            w_main=w_main, w_lr=w_lr, wg=wg, bg=bg, lamvec=lamvec,
            pre_mix_g=row(pre_mix_g), da_norm_g=row(da_norm_g), gla_norm_g=row(gla_norm_g),
            w_out=w_out[l].astype(BF16), post_mix_g=row(post_mix_g), pre_ffn_g=row(pre_ffn_g),
            w_ffn_up=w_ffn_up[l].astype(BF16), ffn_conv_w=ffn_conv_w[l].astype(F32),
            ffn_conv_b=row(ffn_conv_b), w_ffn_down=w_ffn_down[l].astype(BF16),
            post_ffn_g=row(post_ffn_g), w_ple_gate=w_ple_gate[l].astype(BF16),
            b_ple_gate=row(b_ple_gate), w_ple_proj=w_ple_proj[l].astype(BF16),
            ple_norm_g=row(ple_norm_g)))

    def trunk(x, p):
        batch, seq, _ = x.shape
        t = batch * seq
        x2d = x.reshape(t, D_MODEL)
        for l in range(depth):
            lam_init = 0.8 - 0.6 * math.exp(-0.3 * l)
            x2d = _layer(x2d, p[l].reshape(t, PLE_DIM), batch, seq, lam_init, layers[l])
        return x2d.reshape(batch, seq, D_MODEL)

    return (trunk(x_prompt, p_prompt), trunk(x_sample, p_sample))
```

```python
import functools
import math

import jax
import jax.numpy as jnp
from jax import lax
from jax.experimental import pallas as pl
from jax.experimental.pallas import tpu as pltpu

F32 = jnp.float32
BF16 = jnp.bfloat16

D_MODEL = 1024
PLE_DIM = 256
DA_HEADS = 4
DA_QK_DIM = 64
DA_V_DIM = 128
DA_WIDTH = DA_HEADS * DA_V_DIM
GLA_HEADS = 4
GLA_DK = 64
GLA_DV = 128
GLA_QK_WIDTH = GLA_HEADS * GLA_DK
GLA_WIDTH = GLA_HEADS * GLA_DV
GLA_GATE_RANK = 16
GLA_GATE_NORM = 16.0
GLA_CHUNK = 64
D_FF = 4 * D_MODEL
NORM_EPS = 1e-6
LANES = 128
MAIN_WIDTH = 3 * DA_WIDTH + 2 * GLA_QK_WIDTH + 2 * GLA_WIDTH
NEG_BIG = -1e30
LOG2E = math.log2(math.e)
ZERO_PROB_GAP = 150.0
ONES_ROWS = 16
NSLOT = 4
VMEM_LIMIT = 56 * 1024 * 1024

COL_DA_Q = 0
COL_DA_K = DA_WIDTH
COL_DA_V = 2 * DA_WIDTH
COL_GQ = 3 * DA_WIDTH
COL_GK = COL_GQ + GLA_QK_WIDTH
COL_GV = COL_GK + GLA_QK_WIDTH
COL_GR = COL_GV + GLA_WIDTH


def _rms(x, g):
    return x * lax.rsqrt(jnp.mean(x * x, axis=-1, keepdims=True) + NORM_EPS) * g


def _nt_dot(a, b):
    return lax.dot_general(a, b, (((1,), (1,)), ((), ())), preferred_element_type=F32)


def _tn_dot(a, b):
    return lax.dot_general(a, b, (((0,), (0,)), ((), ())), preferred_element_type=F32)


def _dot(a, b):
    return jnp.dot(a, b, preferred_element_type=F32)


def _in_proj_body(x_ref, g_ref, wm_ref, wlr_ref, gsel_ref, proj_ref, qt_ref, vt_ref, lr_ref, kn_ref):
    h = _rms(x_ref[...], g_ref[...]).astype(BF16)
    main = _dot(h, wm_ref[...])
    proj_ref[...] = main.astype(BF16)
    qt_ref[...] = main[:, COL_DA_Q:COL_DA_Q + DA_WIDTH].T.astype(BF16)
    for hd in range(DA_HEADS):
        c0 = COL_DA_V + hd * DA_V_DIM
        vt_ref[hd, :DA_V_DIM, :] = main[:, c0:c0 + DA_V_DIM].T.astype(BF16)
        vt_ref[hd, DA_V_DIM:, :] = jnp.ones((ONES_ROWS, main.shape[0]), BF16)
    lr_ref[...] = _dot(h, wlr_ref[...]).astype(BF16)
    k = main[:, COL_DA_K:COL_DA_K + DA_WIDTH]
    kn_ref[...] = jnp.max(_dot((k * k).astype(BF16), gsel_ref[...]), axis=0, keepdims=True)


def _in_proj(x2d, g, w_main, w_lr, batch, seq, tm):
    t = x2d.shape[0]
    per_seq = seq // tm
    va_rows = DA_V_DIM + ONES_ROWS
    col_group = jnp.arange(DA_WIDTH, dtype=jnp.int32)[:, None] // DA_QK_DIM
    gsel = (col_group == jnp.arange(LANES, dtype=jnp.int32)[None, :]).astype(BF16)
    return pl.pallas_call(
        _in_proj_body,
        grid=(t // tm,),
        in_specs=[
            pl.BlockSpec((tm, D_MODEL), lambda i: (i, 0)),
            pl.BlockSpec((1, D_MODEL), lambda i: (0, 0)),
            pl.BlockSpec((D_MODEL, MAIN_WIDTH), lambda i: (0, 0)),
            pl.BlockSpec((D_MODEL, LANES), lambda i: (0, 0)),
            pl.BlockSpec((DA_WIDTH, LANES), lambda i: (0, 0)),
        ],
        out_specs=[
            pl.BlockSpec((tm, MAIN_WIDTH), lambda i: (i, 0)),
            pl.BlockSpec((None, DA_WIDTH, tm), lambda i: (i // per_seq, 0, i % per_seq)),
            pl.BlockSpec((None, DA_HEADS, va_rows, tm),
                         lambda i: (i // per_seq, 0, 0, i % per_seq)),
            pl.BlockSpec((tm, LANES), lambda i: (i, 0)),
            pl.BlockSpec((None, 1, LANES), lambda i: (i, 0, 0)),
        ],
        out_shape=[
            jax.ShapeDtypeStruct((t, MAIN_WIDTH), BF16),
            jax.ShapeDtypeStruct((batch, DA_WIDTH, seq), BF16),
            jax.ShapeDtypeStruct((batch, DA_HEADS, va_rows, seq), BF16),
            jax.ShapeDtypeStruct((t, LANES), BF16),
            jax.ShapeDtypeStruct((t // tm, 1, LANES), F32),
        ],
        compiler_params=pltpu.CompilerParams(
            dimension_semantics=("arbitrary",), vmem_limit_bytes=VMEM_LIMIT),
        name="in_proj",
    )(x2d, g, w_main, w_lr, gsel)


def _split3(x):
    hi = x.astype(BF16).astype(F32)
    mid = (x - hi).astype(BF16).astype(F32)
    lo = (x - hi - mid).astype(BF16).astype(F32)
    return hi, mid, lo


def _attn_consts(tq):
    i = jnp.arange(tq, dtype=jnp.int32)
    lo = (i & 255).astype(F32)
    hi = (i - (i & 255)).astype(F32)
    slopes = jnp.asarray([2.0 ** (-8.0 * (h + 1) / DA_HEADS) for h in range(DA_HEADS)], F32) * LOG2E
    pieces = jnp.stack([p for piece in _split3(slopes) for p in (piece, piece)], axis=1)
    half = jnp.zeros((DA_HEADS, DA_QK_DIM, tq), F32).at[:, :6].set(
        jnp.broadcast_to(pieces[:, :, None], (DA_HEADS, 6, tq)))
    zeros = jnp.zeros_like(half)
    left = jnp.stack([jnp.concatenate([zeros, half], 1),
                      jnp.concatenate([half, zeros], 1)], axis=1)
    augq = jnp.stack([left, -left], axis=1).astype(BF16)
    kcols = jnp.stack([lo, hi, lo, hi, lo, hi], axis=-1)
    khalf = jnp.zeros((tq, DA_QK_DIM), F32).at[:, :6].set(kcols)
    kz = jnp.zeros_like(khalf)
    augk = jnp.stack([jnp.concatenate([kz, khalf], -1),
                      jnp.concatenate([khalf, kz], -1)], axis=0).astype(BF16)
    qb = slopes[:, None] * i.astype(F32)[None, :]
    qbias = jnp.stack([-qb, qb], axis=1)[:, :, None, :]
    d = (i[None, :] - i[:, None]).astype(F32)
    corr = 2.0 * slopes[:, None, None] * jnp.minimum(d, 0.0)[None]
    return augq, augk, qbias, corr


def _attn_body(lam_ref, qt_ref, k_ref, vt_ref, kn_ref, augq_ref, augk_ref, qbias_ref, corr_ref, gn_ref,
               o_ref, qa_ref, *scratch, tq, nk, nslot, lam_init):
    grab = lambda g: [scratch[2 * (g * nslot + sl):2 * (g * nslot + sl) + 2] for sl in range(nslot)]
    s_ref, p_ref, cmax_ref, alpha_ref = grab(0), grab(1), grab(2), grab(3)
    acc_ref, m_ref = scratch[8 * nslot:8 * nslot + 2], scratch[8 * nslot + 2:8 * nslot + 4]
    h = pl.program_id(1)
    qi = pl.program_id(2)
    slope = jnp.where(h == 0, 2.0 ** -2, jnp.where(h == 1, 2.0 ** -4,
                      jnp.where(h == 2, 2.0 ** -6, 2.0 ** -8))).astype(F32) * LOG2E

    qt = qt_ref[...]
    row = lax.broadcasted_iota(jnp.int32, (LANES, tq), 0)
    for side in range(2):
        qa_ref[2 * side] = jnp.where(row < DA_QK_DIM, qt, augq_ref[side, 0])
        qa_ref[2 * side + 1] = jnp.where(row >= DA_QK_DIM, qt, augq_ref[side, 1])
    for mp in range(2):
        m_ref[mp][...] = jnp.full(m_ref[mp].shape, NEG_BIG, F32)
        acc_ref[mp][...] = jnp.zeros(acc_ref[mp].shape, F32)
    lane = lax.broadcasted_iota(jnp.int32, (tq, LANES), 1)

    def scores(kt, side, slot, diag):
        k = k_ref[pl.ds(pl.multiple_of(kt * tq, tq), tq), :]
        ka = (jnp.where(lane < DA_QK_DIM, k, augk_ref[0]),
              jnp.where(lane >= DA_QK_DIM, k, augk_ref[1]))
        for mp in range(2):
            s = _dot(ka[mp], qa_ref[2 * side + mp])
            if diag:
                s = s + corr_ref[...]
            s_ref[slot][mp][...] = s
            cmax_ref[slot][mp][...] = jnp.max(s, axis=0, keepdims=True) + qbias_ref[side]

    scores(qi, 0, 0, True)

    qf = qt.astype(F32)
    lane1 = lax.broadcasted_iota(jnp.int32, (1, LANES), 1)
    kn2 = jnp.max(kn_ref[...], axis=0)
    gap = None
    for mp in range(2):
        rows = slice(mp * DA_QK_DIM, (mp + 1) * DA_QK_DIM)
        qn2 = jnp.max(jnp.sum(qf[rows] * qf[rows], axis=0, keepdims=True), axis=1, keepdims=True)
        kn2_mp = jnp.max(jnp.where(lane1 == 2 * h + mp, kn2, 0.0), axis=1, keepdims=True)
        m_min = jnp.min(cmax_ref[0][mp][...], axis=1, keepdims=True)
        g = 1.02 * jnp.sqrt(qn2 * kn2_mp) - m_min
        gap = g if gap is None else jnp.maximum(gap, g)
    far = ((gap + ZERO_PROB_GAP) / slope - 1.0) / tq
    reach = jnp.where(far < nk, jnp.maximum(jnp.ceil(far), 0.0), float(nk)).astype(jnp.int32)[0, 0]
    n_left = jnp.minimum(qi, reach)
    n_tiles = n_left + jnp.minimum(nk - 1 - qi, reach)

    def key_tile(t):
        return jnp.where(t <= n_left, qi - t, qi + (t - n_left))

    def tile_side(t):
        return (t > n_left).astype(jnp.int32)

    def tile_const(t):
        return -slope * (jnp.abs(qi - key_tile(t)) * tq).astype(F32)

    def probs(t, slot):
        shift = tile_const(t) + qbias_ref[tile_side(t)]
        for mp in range(2):
            m_old = m_ref[mp][...]
            m_new = jnp.maximum(m_old, cmax_ref[slot][mp][...] + tile_const(t))
            alpha_ref[slot][mp][...] = jnp.exp2(m_old - m_new)
            p_ref[slot][mp][...] = jnp.exp2(s_ref[slot][mp][...] - (m_new - shift)).astype(BF16)
            m_ref[mp][...] = m_new

    def accumulate(t, slot):
        va = vt_ref[:, pl.ds(pl.multiple_of(key_tile(t) * tq, tq), tq)]
        for mp in range(2):
            acc_ref[mp][...] = (alpha_ref[slot][mp][...] * acc_ref[mp][...]
                                + _dot(va, p_ref[slot][mp][...]))

    def step(t, slot):
        prev = (slot - 1) % nslot
        probs(t - 1, prev)
        scores(key_tile(t), tile_side(t), slot, False)
        accumulate(t - 1, prev)

    def trip(i, carry):
        for r in range(1, nslot + 1):
            step(nslot * i + r, r % nslot)
        return carry

    lax.fori_loop(0, n_tiles // nslot, trip, 0)
    rem = n_tiles % nslot
    for r in range(1, nslot):
        @pl.when(rem >= r)
        def _(r=r):
            step(n_tiles - rem + r, r)

    for r in range(nslot):
        @pl.when(rem == r)
        def _(r=r):
            probs(n_tiles, r)
            accumulate(n_tiles, r)

    lv = lam_ref[...]
    lam = (jnp.exp(jnp.sum(lv[0:1] * lv[1:2], axis=-1, keepdims=True))
           - jnp.exp(jnp.sum(lv[2:3] * lv[3:4], axis=-1, keepdims=True)) + lam_init)
    a0 = acc_ref[0][...]
    a1 = acc_ref[1][...]
    o0 = a0[:DA_V_DIM] * (1.0 / a0[DA_V_DIM:DA_V_DIM + 1])
    o1 = a1[:DA_V_DIM] * (1.0 / a1[DA_V_DIM:DA_V_DIM + 1])
    out = o0 - lam * o1
    inv = lax.rsqrt(jnp.mean(out * out, axis=0, keepdims=True) + NORM_EPS)
    out = out * inv * gn_ref[...] * (1.0 - lam_init)
    o_ref[...] = out.T.astype(o_ref.dtype)


def _diff_attention(proj, q_t, v_t, kn, lamvec, gn, batch, seq, tq, lam_init):
    t = proj.shape[0]
    nk = seq // tq
    augq, augk, qbias, corr = _attn_consts(tq)
    gnb = jnp.broadcast_to(gn.reshape(DA_V_DIM, 1), (DA_V_DIM, tq))
    nslot = _attn_slots(seq, tq)
    body = functools.partial(_attn_body, tq=tq, nk=nk, nslot=nslot, lam_init=lam_init)
    kblk = COL_DA_K // LANES
    return pl.pallas_call(
        body,
        grid=(batch, DA_HEADS, nk),
        in_specs=[
            pl.BlockSpec((8, LANES), lambda b, h, i: (0, 0)),
            pl.BlockSpec((None, LANES, tq), lambda b, h, i: (b, h, i)),
            pl.BlockSpec((seq, LANES), lambda b, h, i: (b, kblk + h)),
            pl.BlockSpec((None, None, DA_V_DIM + ONES_ROWS, seq), lambda b, h, i: (b, h, 0, 0)),
            pl.BlockSpec((kn.shape[0] // batch, 1, LANES), lambda b, h, i: (b, 0, 0)),
            pl.BlockSpec((None, 2, 2, LANES, tq), lambda b, h, i: (h, 0, 0, 0, 0)),
            pl.BlockSpec((2, tq, LANES), lambda b, h, i: (0, 0, 0)),
            pl.BlockSpec((None, 2, 1, tq), lambda b, h, i: (h, 0, 0, 0)),
            pl.BlockSpec((None, tq, tq), lambda b, h, i: (h, 0, 0)),
            pl.BlockSpec((DA_V_DIM, tq), lambda b, h, i: (0, 0)),
        ],
        out_specs=pl.BlockSpec((tq, LANES), lambda b, h, i: (b * nk + i, h)),
        out_shape=jax.ShapeDtypeStruct((t, DA_WIDTH), BF16),
        scratch_shapes=[
            pltpu.VMEM((4, LANES, tq), BF16),
        ] + [pltpu.VMEM((tq, tq), F32)] * (2 * nslot)
        + [pltpu.VMEM((tq, tq), BF16)] * (2 * nslot)
        + [pltpu.VMEM((1, tq), F32)] * (2 * nslot)
        + [pltpu.VMEM((1, tq), F32)] * (2 * nslot)
        + [pltpu.VMEM((DA_V_DIM + ONES_ROWS, tq), F32)] * 2
        + [pltpu.VMEM((1, tq), F32)] * 2,
        compiler_params=pltpu.CompilerParams(
            dimension_semantics=("arbitrary", "arbitrary", "arbitrary"),
            vmem_limit_bytes=VMEM_LIMIT),
        name="diff_attention",
    )(lamvec, q_t, proj, v_t, kn, augq, augk, qbias, corr, gnb)


def _log_sigmoid(z):
    return jnp.minimum(z, 0.0) - jnp.log1p(jnp.exp(-jnp.abs(z)))


def _gla_chunk(q, k, v, g, st_ref, reverse):
    c = GLA_CHUNK
    q = q.astype(F32)
    k = k.astype(F32)
    row = lax.broadcasted_iota(jnp.int32, (c, GLA_QK_WIDTH), 0)
    lane = lax.broadcasted_iota(jnp.int32, (c, GLA_QK_WIDTH), 1)
    b = g
    shift = 1
    while shift < c:
        b = b + jnp.where(row >= shift, pltpu.roll(b, shift, 0), 0.0)
        shift *= 2
    tot = b[c - 1:c, :]
    if reverse:
        b = tot - b + g
    q_t = (q * jnp.exp(b)).astype(BF16)
    k_t = (k * jnp.exp(-b)).astype(BF16)
    k_end = (k * jnp.exp(tot - b)).astype(BF16)
    decay = jnp.exp(tot)

    head = lane // GLA_DK
    zero_k = jnp.zeros_like(k_t)
    k_bd = jnp.concatenate([jnp.where(head == h, k_t, zero_k) for h in range(GLA_HEADS)], axis=0)
    att = _nt_dot(q_t, k_bd)
    pos = lane % c
    keep = (pos > row) if reverse else (pos <= row)
    att = jnp.where(keep, att, 0.0).astype(BF16)

    vhead = lax.broadcasted_iota(jnp.int32, (c, GLA_WIDTH), 1) // GLA_DV
    zero_v = jnp.zeros_like(v)
    v_bd = jnp.concatenate([jnp.where(vhead == h, v, zero_v) for h in range(GLA_HEADS)], axis=0)
    st = st_ref[...]
    out = _dot(att, v_bd) + _nt_dot(q_t, st.astype(BF16))

    kv_t = _tn_dot(v, k_end)
    r_head = lax.broadcasted_iota(jnp.int32, kv_t.shape, 0) // GLA_DV
    c_head = lax.broadcasted_iota(jnp.int32, kv_t.shape, 1) // GLA_DK
    st_ref[...] = decay * st + jnp.where(r_head == c_head, kv_t, 0.0)
    return out


def _gla_body(qf_ref, kf_ref, vf_ref, lrf_ref, qb_ref, kb_ref, vb_ref, lrb_ref, wg_ref, bg_ref,
              of_ref, ob_ref, stf_ref, stb_ref, *, rows):
    @pl.when(pl.program_id(1) == 0)
    def _():
        stf_ref[...] = jnp.zeros(stf_ref.shape, F32)
        stb_ref[...] = jnp.zeros(stb_ref.shape, F32)

    wg = wg_ref[...]
    bg = bg_ref[...]
    zf = _dot(lrf_ref[...], wg[:, :GLA_QK_WIDTH]) + bg[:, :GLA_QK_WIDTH]
    zb = _dot(lrb_ref[...], wg[:, GLA_QK_WIDTH:]) + bg[:, GLA_QK_WIDTH:]
    gf = _log_sigmoid(zf) / GLA_GATE_NORM
    gb = _log_sigmoid(zb) / GLA_GATE_NORM
    nc = rows // GLA_CHUNK
    for ci in range(nc):
        sf = slice(ci * GLA_CHUNK, (ci + 1) * GLA_CHUNK)
        of_ref[sf, :] = _gla_chunk(qf_ref[sf, :], kf_ref[sf, :], vf_ref[sf, :], gf[sf, :],
                                   stf_ref, False)
        cj = nc - 1 - ci
        sb = slice(cj * GLA_CHUNK, (cj + 1) * GLA_CHUNK)
        ob_ref[sb, :] = _gla_chunk(qb_ref[sb, :], kb_ref[sb, :], vb_ref[sb, :], gb[sb, :],
                                   stb_ref, True)


def _gla(proj, lr, wg, bg, batch, seq, rows):
    t = proj.shape[0]
    nb = seq // rows
    qblk = COL_GQ // GLA_QK_WIDTH
    kblk = COL_GK // GLA_QK_WIDTH
    vblk = COL_GV // GLA_WIDTH

    def fwd(col):
        return lambda b, i: (b * nb + i, col)

    def bwd(col):
        return lambda b, i: (b * nb + nb - 1 - i, col)

    def specs(ix):
        return [
            pl.BlockSpec((rows, GLA_QK_WIDTH), ix(qblk)),
            pl.BlockSpec((rows, GLA_QK_WIDTH), ix(kblk)),
            pl.BlockSpec((rows, GLA_WIDTH), ix(vblk)),
            pl.BlockSpec((rows, LANES), ix(0)),
        ]

    return pl.pallas_call(
        functools.partial(_gla_body, rows=rows),
        grid=(batch, nb),
        in_specs=specs(fwd) + specs(bwd) + [
            pl.BlockSpec((LANES, 2 * GLA_QK_WIDTH), lambda b, i: (0, 0)),
            pl.BlockSpec((1, 2 * GLA_QK_WIDTH), lambda b, i: (0, 0)),
        ],
        out_specs=[
            pl.BlockSpec((rows, GLA_WIDTH), fwd(0)),
            pl.BlockSpec((rows, GLA_WIDTH), bwd(0)),
        ],
        out_shape=[jax.ShapeDtypeStruct((t, GLA_WIDTH), F32)] * 2,
        scratch_shapes=[pltpu.VMEM((GLA_WIDTH, GLA_QK_WIDTH), F32)] * 2,
        compiler_params=pltpu.CompilerParams(
            dimension_semantics=("arbitrary", "arbitrary"), vmem_limit_bytes=VMEM_LIMIT),
        name="gla",
    )(proj, proj, proj, lr, proj, proj, proj, lr, wg, bg)


def _out_proj_body(da_ref, of_ref, ob_ref, gr_ref, x_ref, gng_ref, wout_ref, pmg_ref, pfg_ref,
                   x1_ref, h2_ref):
    o = of_ref[...] + ob_ref[...]
    gr = gr_ref[...].astype(F32)
    parts = []
    for h in range(GLA_HEADS):
        sl = slice(h * GLA_DV, (h + 1) * GLA_DV)
        gh = gr[:, sl]
        parts.append(_rms(o[:, sl], gng_ref[...]) * (gh * jax.nn.sigmoid(gh)))
    gla = jnp.concatenate(parts, axis=1).astype(BF16)
    mix = _dot(jnp.concatenate([da_ref[...], gla], axis=1), wout_ref[...])
    x1 = x_ref[...] + _rms(mix, pmg_ref[...])
    x1_ref[...] = x1
    h2_ref[...] = _rms(x1, pfg_ref[...]).astype(BF16)


def _out_proj(da, o_f, o_b, proj, x2d, gng, w_out, pmg, pfg, tm):
    t = x2d.shape[0]
    row = lambda i: (i, 0)
    const = lambda i: (0, 0)
    return pl.pallas_call(
        _out_proj_body,
        grid=(t // tm,),
        in_specs=[
            pl.BlockSpec((tm, DA_WIDTH), row),
            pl.BlockSpec((tm, GLA_WIDTH), row),
            pl.BlockSpec((tm, GLA_WIDTH), row),
            pl.BlockSpec((tm, GLA_WIDTH), lambda i: (i, COL_GR // GLA_WIDTH)),
            pl.BlockSpec((tm, D_MODEL), row),
            pl.BlockSpec((1, GLA_DV), const),
            pl.BlockSpec((D_MODEL, D_MODEL), const),
            pl.BlockSpec((1, D_MODEL), const),
            pl.BlockSpec((1, D_MODEL), const),
        ],
        out_specs=[pl.BlockSpec((tm, D_MODEL), row), pl.BlockSpec((tm, D_MODEL), row)],
        out_shape=[jax.ShapeDtypeStruct((t, D_MODEL), F32),
                   jax.ShapeDtypeStruct((t, D_MODEL), BF16)],
        compiler_params=pltpu.CompilerParams(
            dimension_semantics=("arbitrary",), vmem_limit_bytes=VMEM_LIMIT),
        name="out_proj",
    )(da, o_f, o_b, proj, x2d, gng, w_out, pmg, pfg)


HALO = 16


def _ffn_body(h_ref, hp_ref, hn_ref, wg_ref, wu_ref, cw_ref, cb_ref, wd_ref, x1_ref, p_ref,
              pfg_ref, wpg_ref, bpg_ref, wpp_ref, png_ref, o_ref, hext_ref, act_ref,
              *, tm, tf, seq, nff):
    i = pl.program_id(0)
    j = pl.program_id(1)

    @pl.when(j == 0)
    def _():
        first = (i * tm) % seq == 0
        last = ((i + 1) * tm) % seq == 0
        hp = hp_ref[...]
        hn = hn_ref[...]
        hext_ref[0:HALO, :] = jnp.where(first, jnp.zeros_like(hp), hp)
        hext_ref[HALO:HALO + tm, :] = h_ref[...]
        hext_ref[HALO + tm:, :] = jnp.where(last, jnp.zeros_like(hn), hn)

    gate = _dot(hext_ref[...], wg_ref[...])
    ext = tm + 2 * HALO
    cw = cw_ref[...]
    conv = (pltpu.roll(gate, 1, 0)[HALO:HALO + tm] * cw[0:1]
            + gate[HALO:HALO + tm] * cw[1:2]
            + pltpu.roll(gate, ext - 1, 0)[HALO:HALO + tm] * cw[2:3]
            + cb_ref[...])
    up = _dot(h_ref[...], wu_ref[...])
    act_ref[:, pl.ds(pl.multiple_of(j * tf, tf), tf)] = (jax.nn.gelu(conv) * up).astype(BF16)

    @pl.when(j == nff - 1)
    def _():
        x2 = x1_ref[...] + _rms(_dot(act_ref[...], wd_ref[...]), pfg_ref[...])
        e = _rms(_dot(p_ref[...].astype(BF16), wpp_ref[...]), png_ref[...])
        gate_e = jax.nn.sigmoid(_dot(x2.astype(BF16), wpg_ref[...]) + bpg_ref[...])
        o_ref[...] = x2 + gate_e * e


def _ffn(h2, x1, p2d, w_up, conv_w, conv_b, w_down, pfg, w_pg, b_pg, w_pp, png, seq, tm, tf):
    t = h2.shape[0]
    nff = D_FF // tf
    nhalo = t // HALO
    per = tm // HALO
    row = lambda i, j: (i, 0)
    const = lambda i, j: (0, 0)
    return pl.pallas_call(
        functools.partial(_ffn_body, tm=tm, tf=tf, seq=seq, nff=nff),
        grid=(t // tm, nff),
        in_specs=[
            pl.BlockSpec((tm, D_MODEL), row),
            pl.BlockSpec((HALO, D_MODEL), lambda i, j: (jnp.maximum(i * per - 1, 0), 0)),
            pl.BlockSpec((HALO, D_MODEL), lambda i, j: (jnp.minimum((i + 1) * per, nhalo - 1), 0)),
            pl.BlockSpec((D_MODEL, tf), lambda i, j: (0, j)),
            pl.BlockSpec((D_MODEL, tf), lambda i, j: (0, nff + j)),
            pl.BlockSpec((3, tf), lambda i, j: (0, j)),
            pl.BlockSpec((1, tf), lambda i, j: (0, j)),
            pl.BlockSpec((D_FF, D_MODEL), const, pipeline_mode=pl.Buffered(1)),
            pl.BlockSpec((tm, D_MODEL), row),
            pl.BlockSpec((tm, PLE_DIM), row),
            pl.BlockSpec((1, D_MODEL), const),
            pl.BlockSpec((D_MODEL, D_MODEL), const, pipeline_mode=pl.Buffered(1)),
            pl.BlockSpec((1, D_MODEL), const),
            pl.BlockSpec((PLE_DIM, D_MODEL), const, pipeline_mode=pl.Buffered(1)),
            pl.BlockSpec((1, D_MODEL), const),
        ],
        out_specs=pl.BlockSpec((tm, D_MODEL), row),
        out_shape=jax.ShapeDtypeStruct((t, D_MODEL), F32),
        scratch_shapes=[
            pltpu.VMEM((tm + 2 * HALO, D_MODEL), BF16),
            pltpu.VMEM((tm, D_FF), BF16),
        ],
        compiler_params=pltpu.CompilerParams(
            dimension_semantics=("arbitrary", "arbitrary"), vmem_limit_bytes=VMEM_LIMIT),
        name="conv_ffn",
    )(h2, h2, h2, w_up, w_up, conv_w, conv_b, w_down, x1, p2d, pfg, w_pg, b_pg, w_pp, png)


def _prep_weights(l, w_in, gla_w_gate_f, gla_b_gate_f, gla_w_gate_b, gla_b_gate_b, da_lq1,
                  da_lk1, da_lq2, da_lk2):
    w = w_in[l]
    qscale = jnp.concatenate([
        jnp.full((DA_WIDTH,), DA_QK_DIM ** -0.5 * LOG2E, F32), jnp.ones((2 * DA_WIDTH,), F32),
        jnp.full((GLA_QK_WIDTH,), GLA_DK ** -0.5, F32),
        jnp.ones((GLA_QK_WIDTH + 2 * GLA_WIDTH,), F32)])
    w_main = (w[:, :MAIN_WIDTH] * qscale).astype(BF16)
    w_lr = jnp.zeros((D_MODEL, LANES), F32).at[:, :2 * GLA_GATE_RANK].set(w[:, MAIN_WIDTH:])
    wg = jnp.zeros((LANES, 2 * GLA_QK_WIDTH), F32)
    wg = wg.at[:GLA_GATE_RANK, :GLA_QK_WIDTH].set(gla_w_gate_f[l])
    wg = wg.at[GLA_GATE_RANK:2 * GLA_GATE_RANK, GLA_QK_WIDTH:].set(gla_w_gate_b[l])
    bg = jnp.concatenate([gla_b_gate_f[l], gla_b_gate_b[l]])[None, :]
    lamvec = jnp.zeros((8, LANES), F32)
    for r, vec in enumerate((da_lq1, da_lk1, da_lq2, da_lk2)):
        lamvec = lamvec.at[r, :DA_QK_DIM].set(vec[l].astype(F32))
    return w_main, w_lr.astype(BF16), wg.astype(BF16), bg, lamvec


ROW_TILE = 512
ATTN_TILE = 1024
GLA_ROWS = 256
FF_TILE = 2048


def _attn_slots(seq, tile):
    return min(NSLOT, max(2, seq // tile))


def _attn_tile(seq):
    tile = min(seq, ATTN_TILE)
    while tile > LANES:
        resident = 2 * 2 * seq * (LANES + DA_V_DIM + ONES_ROWS)
        tiles = tile * tile * (2 * _attn_slots(seq, tile) * (4 + 2) + 2 * 4)
        if resident + tiles <= VMEM_LIMIT * 3 // 4:
            break
        tile //= 2
    return tile


def _layer(x2d, p2d, batch, seq, lam_init, wts):
    tm = min(seq, ROW_TILE)
    proj, q_t, v_t, lr, kn = _in_proj(x2d, wts["pre_mix_g"], wts["w_main"], wts["w_lr"], batch, seq, tm)
    da = _diff_attention(proj, q_t, v_t, kn, wts["lamvec"], wts["da_norm_g"], batch, seq,
                         _attn_tile(seq), lam_init)
    o_f, o_b = _gla(proj, lr, wts["wg"], wts["bg"], batch, seq, min(seq, GLA_ROWS))
    x1, h2 = _out_proj(da, o_f, o_b, proj, x2d, wts["gla_norm_g"], wts["w_out"],
                       wts["post_mix_g"], wts["pre_ffn_g"], tm)
    return _ffn(h2, x1, p2d, wts["w_ffn_up"], wts["ffn_conv_w"], wts["ffn_conv_b"],
                wts["w_ffn_down"], wts["post_ffn_g"], wts["w_ple_gate"], wts["b_ple_gate"],
                wts["w_ple_proj"], wts["ple_norm_g"], seq, tm, FF_TILE)


def kernel(x_prompt, x_sample, p_prompt, p_sample, pre_mix_g, w_in, da_lq1, da_lk1, da_lq2, da_lk2, da_norm_g, gla_w_gate_f, gla_b_gate_f, gla_w_gate_b, gla_b_gate_b, gla_norm_g, w_out, post_mix_g, pre_ffn_g, w_ffn_up, ffn_conv_w, ffn_conv_b, w_ffn_down, post_ffn_g, w_ple_gate, b_ple_gate, w_ple_proj, ple_norm_g):
    depth = w_in.shape[0]
    layers = []
    for l in range(depth):
        w_main, w_lr, wg, bg, lamvec = _prep_weights(
            l, w_in, gla_w_gate_f, gla_b_gate_f, gla_w_gate_b, gla_b_gate_b,
            da_lq1, da_lk1, da_lq2, da_lk2)
        row = lambda a: a[l][None, :].astype(F32)
        layers.append(dict(
            w_main=w_main, w_lr=w_lr, wg=wg, bg=bg, lamvec=lamvec,
            pre_mix_g=row(pre_mix_g), da_norm_g=row(da_norm_g), gla_norm_g=row(gla_norm_g),
            w_out=w_out[l].astype(BF16), post_mix_g=row(post_mix_g), pre_ffn_g=row(pre_ffn_g),
            w_ffn_up=w_ffn_up[l].astype(BF16), ffn_conv_w=ffn_conv_w[l].astype(F32),
            ffn_conv_b=row(ffn_conv_b), w_ffn_down=w_ffn_down[l].astype(BF16),
            post_ffn_g=row(post_ffn_g), w_ple_gate=w_ple_gate[l].astype(BF16),
            b_ple_gate=row(b_ple_gate), w_ple_proj=w_ple_proj[l].astype(BF16),
            ple_norm_g=row(ple_norm_g)))

    def trunk(x, p):
        batch, seq, _ = x.shape
        t = batch * seq
        x2d = x.reshape(t, D_MODEL)
        for l in range(depth):
            lam_init = 0.8 - 0.6 * math.exp(-0.3 * l)
            x2d = _layer(x2d, p[l].reshape(t, PLE_DIM), batch, seq, lam_init, layers[l])
        return x2d.reshape(batch, seq, D_MODEL)

    return (trunk(x_prompt, p_prompt), trunk(x_sample, p_sample))
```

```python
import functools
import math

import jax
import jax.numpy as jnp
from jax import lax
from jax.experimental import pallas as pl
from jax.experimental.pallas import tpu as pltpu

F32 = jnp.float32
BF16 = jnp.bfloat16

D_MODEL = 1024
PLE_DIM = 256
DA_HEADS = 4
DA_QK_DIM = 64
DA_V_DIM = 128
DA_WIDTH = DA_HEADS * DA_V_DIM
GLA_HEADS = 4
GLA_DK = 64
GLA_DV = 128
GLA_QK_WIDTH = GLA_HEADS * GLA_DK
GLA_WIDTH = GLA_HEADS * GLA_DV
GLA_GATE_RANK = 16
GLA_GATE_NORM = 16.0
GLA_CHUNK = 64
D_FF = 4 * D_MODEL
NORM_EPS = 1e-6
LANES = 128
MAIN_WIDTH = 3 * DA_WIDTH + 2 * GLA_QK_WIDTH + 2 * GLA_WIDTH
NEG_BIG = -1e30
LOG2E = math.log2(math.e)
ZERO_PROB_GAP = 150.0
ONES_ROWS = 16
NSLOT = 2
VMEM_LIMIT = 56 * 1024 * 1024

COL_DA_Q = 0
COL_DA_K = DA_WIDTH
COL_DA_V = 2 * DA_WIDTH
COL_GQ = 3 * DA_WIDTH
COL_GK = COL_GQ + GLA_QK_WIDTH
COL_GV = COL_GK + GLA_QK_WIDTH
COL_GR = COL_GV + GLA_WIDTH


def _rms(x, g):
    return x * lax.rsqrt(jnp.mean(x * x, axis=-1, keepdims=True) + NORM_EPS) * g


def _nt_dot(a, b):
    return lax.dot_general(a, b, (((1,), (1,)), ((), ())), preferred_element_type=F32)


def _tn_dot(a, b):
    return lax.dot_general(a, b, (((0,), (0,)), ((), ())), preferred_element_type=F32)


def _dot(a, b):
    return jnp.dot(a, b, preferred_element_type=F32)


def _in_proj_body(x_ref, g_ref, wm_ref, wlr_ref, gsel_ref, proj_ref, qt_ref, vt_ref, lr_ref, kn_ref):
    h = _rms(x_ref[...], g_ref[...]).astype(BF16)
    main = _dot(h, wm_ref[...])
    proj_ref[...] = main.astype(BF16)
    qt_ref[...] = main[:, COL_DA_Q:COL_DA_Q + DA_WIDTH].T.astype(BF16)
    for hd in range(DA_HEADS):
        c0 = COL_DA_V + hd * DA_V_DIM
        vt_ref[hd, :DA_V_DIM, :] = main[:, c0:c0 + DA_V_DIM].T.astype(BF16)
        vt_ref[hd, DA_V_DIM:, :] = jnp.ones((ONES_ROWS, main.shape[0]), BF16)
    lr_ref[...] = _dot(h, wlr_ref[...]).astype(BF16)
    k = main[:, COL_DA_K:COL_DA_K + DA_WIDTH]
    kn_ref[...] = jnp.max(_dot((k * k).astype(BF16), gsel_ref[...]), axis=0, keepdims=True)


def _in_proj(x2d, g, w_main, w_lr, batch, seq, tm):
    t = x2d.shape[0]
    per_seq = seq // tm
    va_rows = DA_V_DIM + ONES_ROWS
    col_group = jnp.arange(DA_WIDTH, dtype=jnp.int32)[:, None] // DA_QK_DIM
    gsel = (col_group == jnp.arange(LANES, dtype=jnp.int32)[None, :]).astype(BF16)
    return pl.pallas_call(
        _in_proj_body,
        grid=(t // tm,),
        in_specs=[
            pl.BlockSpec((tm, D_MODEL), lambda i: (i, 0)),
            pl.BlockSpec((1, D_MODEL), lambda i: (0, 0)),
            pl.BlockSpec((D_MODEL, MAIN_WIDTH), lambda i: (0, 0)),
            pl.BlockSpec((D_MODEL, LANES), lambda i: (0, 0)),
            pl.BlockSpec((DA_WIDTH, LANES), lambda i: (0, 0)),
        ],
        out_specs=[
            pl.BlockSpec((tm, MAIN_WIDTH), lambda i: (i, 0)),
            pl.BlockSpec((None, DA_WIDTH, tm), lambda i: (i // per_seq, 0, i % per_seq)),
            pl.BlockSpec((None, DA_HEADS, va_rows, tm),
                         lambda i: (i // per_seq, 0, 0, i % per_seq)),
            pl.BlockSpec((tm, LANES), lambda i: (i, 0)),
            pl.BlockSpec((None, 1, LANES), lambda i: (i, 0, 0)),
        ],
        out_shape=[
            jax.ShapeDtypeStruct((t, MAIN_WIDTH), BF16),
            jax.ShapeDtypeStruct((batch, DA_WIDTH, seq), BF16),
            jax.ShapeDtypeStruct((batch, DA_HEADS, va_rows, seq), BF16),
            jax.ShapeDtypeStruct((t, LANES), BF16),
            jax.ShapeDtypeStruct((t // tm, 1, LANES), F32),
        ],
        compiler_params=pltpu.CompilerParams(
            dimension_semantics=("arbitrary",), vmem_limit_bytes=VMEM_LIMIT),
        name="in_proj",
    )(x2d, g, w_main, w_lr, gsel)


def _split3(x):
    hi = x.astype(BF16).astype(F32)
    mid = (x - hi).astype(BF16).astype(F32)
    lo = (x - hi - mid).astype(BF16).astype(F32)
    return hi, mid, lo


def _attn_consts(tq):
    i = jnp.arange(tq, dtype=jnp.int32)
    lo = (i & 255).astype(F32)
    hi = (i - (i & 255)).astype(F32)
    slopes = jnp.asarray([2.0 ** (-8.0 * (h + 1) / DA_HEADS) for h in range(DA_HEADS)], F32) * LOG2E
    pieces = jnp.stack([p for piece in _split3(slopes) for p in (piece, piece)], axis=1)
    half = jnp.zeros((DA_HEADS, DA_QK_DIM, tq), F32).at[:, :6].set(
        jnp.broadcast_to(pieces[:, :, None], (DA_HEADS, 6, tq)))
    zeros = jnp.zeros_like(half)
    left = jnp.stack([jnp.concatenate([zeros, half], 1),
                      jnp.concatenate([half, zeros], 1)], axis=1)
    augq = jnp.stack([left, -left], axis=1).astype(BF16)
    kcols = jnp.stack([lo, hi, lo, hi, lo, hi], axis=-1)
    khalf = jnp.zeros((tq, DA_QK_DIM), F32).at[:, :6].set(kcols)
    kz = jnp.zeros_like(khalf)
    augk = jnp.stack([jnp.concatenate([kz, khalf], -1),
                      jnp.concatenate([khalf, kz], -1)], axis=0).astype(BF16)
    qb = slopes[:, None] * i.astype(F32)[None, :]
    qbias = jnp.stack([-qb, qb], axis=1)[:, :, None, :]
    d = (i[None, :] - i[:, None]).astype(F32)
    corr = 2.0 * slopes[:, None, None] * jnp.minimum(d, 0.0)[None]
    return augq, augk, qbias, corr


def _attn_body(lam_ref, qt_ref, k_ref, vt_ref, kn_ref, augq_ref, augk_ref, qbias_ref, corr_ref, gn_ref,
               o_ref, qa_ref, *scratch, tq, nk, nslot, lam_init):
    grab = lambda g: [scratch[2 * (g * nslot + sl):2 * (g * nslot + sl) + 2] for sl in range(nslot)]
    s_ref, p_ref, cmax_ref, alpha_ref = grab(0), grab(1), grab(2), grab(3)
    acc_ref, m_ref = scratch[8 * nslot:8 * nslot + 2], scratch[8 * nslot + 2:8 * nslot + 4]
    h = pl.program_id(1)
    qi = pl.program_id(2)
    slope = jnp.where(h == 0, 2.0 ** -2, jnp.where(h == 1, 2.0 ** -4,
                      jnp.where(h == 2, 2.0 ** -6, 2.0 ** -8))).astype(F32) * LOG2E

    qt = qt_ref[...]
    row = lax.broadcasted_iota(jnp.int32, (LANES, tq), 0)
    for side in range(2):
        qa_ref[2 * side] = jnp.where(row < DA_QK_DIM, qt, augq_ref[side, 0])
        qa_ref[2 * side + 1] = jnp.where(row >= DA_QK_DIM, qt, augq_ref[side, 1])
    for mp in range(2):
        m_ref[mp][...] = jnp.full(m_ref[mp].shape, NEG_BIG, F32)
        acc_ref[mp][...] = jnp.zeros(acc_ref[mp].shape, F32)
    lane = lax.broadcasted_iota(jnp.int32, (tq, LANES), 1)

    def scores(kt, side, slot, diag):
        k = k_ref[pl.ds(pl.multiple_of(kt * tq, tq), tq), :]
        ka = (jnp.where(lane < DA_QK_DIM, k, augk_ref[0]),
              jnp.where(lane >= DA_QK_DIM, k, augk_ref[1]))
        for mp in range(2):
            s = _dot(ka[mp], qa_ref[2 * side + mp])
            if diag:
                s = s + corr_ref[...]
            s_ref[slot][mp][...] = s
            cmax_ref[slot][mp][...] = jnp.max(s, axis=0, keepdims=True) + qbias_ref[side]

    scores(qi, 0, 0, True)

    qf = qt.astype(F32)
    lane1 = lax.broadcasted_iota(jnp.int32, (1, LANES), 1)
    kn2 = jnp.max(kn_ref[...], axis=0)
    gap = None
    for mp in range(2):
        rows = slice(mp * DA_QK_DIM, (mp + 1) * DA_QK_DIM)
        qn2 = jnp.max(jnp.sum(qf[rows] * qf[rows], axis=0, keepdims=True), axis=1, keepdims=True)
        kn2_mp = jnp.max(jnp.where(lane1 == 2 * h + mp, kn2, 0.0), axis=1, keepdims=True)
        m_min = jnp.min(cmax_ref[0][mp][...], axis=1, keepdims=True)
        g = 1.02 * jnp.sqrt(qn2 * kn2_mp) - m_min
        gap = g if gap is None else jnp.maximum(gap, g)
    far = ((gap + ZERO_PROB_GAP) / slope - 1.0) / tq
    reach = jnp.where(far < nk, jnp.maximum(jnp.ceil(far), 0.0), float(nk)).astype(jnp.int32)[0, 0]
    n_left = jnp.minimum(qi, reach)
    n_tiles = n_left + jnp.minimum(nk - 1 - qi, reach)

    def key_tile(t):
        return jnp.where(t <= n_left, qi - t, qi + (t - n_left))

    def tile_side(t):
        return (t > n_left).astype(jnp.int32)

    def tile_const(t):
        return -slope * (jnp.abs(qi - key_tile(t)) * tq).astype(F32)

    def probs(t, slot):
        shift = tile_const(t) + qbias_ref[tile_side(t)]
        for mp in range(2):
            m_old = m_ref[mp][...]
            m_new = jnp.maximum(m_old, cmax_ref[slot][mp][...] + tile_const(t))
            alpha_ref[slot][mp][...] = jnp.exp2(m_old - m_new)
            p_ref[slot][mp][...] = jnp.exp2(s_ref[slot][mp][...] - (m_new - shift)).astype(BF16)
            m_ref[mp][...] = m_new

    def accumulate(t, slot):
        va = vt_ref[:, pl.ds(pl.multiple_of(key_tile(t) * tq, tq), tq)]
        for mp in range(2):
            acc_ref[mp][...] = (alpha_ref[slot][mp][...] * acc_ref[mp][...]
                                + _dot(va, p_ref[slot][mp][...]))

    def step(t, slot):
        prev = (slot - 1) % nslot

        @pl.when(t <= n_tiles)
        def _():
            probs(t - 1, prev)
            scores(key_tile(t), tile_side(t), slot, False)
            accumulate(t - 1, prev)

    def trip(i, carry):
        for r in range(1, nslot + 1):
            step(nslot * i + r, r % nslot)
        return carry

    lax.fori_loop(0, (n_tiles + nslot - 1) // nslot, trip, 0)
    for r in range(nslot):
        @pl.when(n_tiles % nslot == r)
        def _(r=r):
            probs(n_tiles, r)
            accumulate(n_tiles, r)

    lv = lam_ref[...]
    lam = (jnp.exp(jnp.sum(lv[0:1] * lv[1:2], axis=-1, keepdims=True))
           - jnp.exp(jnp.sum(lv[2:3] * lv[3:4], axis=-1, keepdims=True)) + lam_init)
    a0 = acc_ref[0][...]
    a1 = acc_ref[1][...]
    o0 = a0[:DA_V_DIM] * (1.0 / a0[DA_V_DIM:DA_V_DIM + 1])
    o1 = a1[:DA_V_DIM] * (1.0 / a1[DA_V_DIM:DA_V_DIM + 1])
    out = o0 - lam * o1
    inv = lax.rsqrt(jnp.mean(out * out, axis=0, keepdims=True) + NORM_EPS)
    out = out * inv * gn_ref[...] * (1.0 - lam_init)
    o_ref[...] = out.T.astype(o_ref.dtype)


def _diff_attention(proj, q_t, v_t, kn, lamvec, gn, batch, seq, tq, lam_init):
    t = proj.shape[0]
    nk = seq // tq
    augq, augk, qbias, corr = _attn_consts(tq)
    gnb = jnp.broadcast_to(gn.reshape(DA_V_DIM, 1), (DA_V_DIM, tq))
    nslot = _attn_slots(seq, tq)
    body = functools.partial(_attn_body, tq=tq, nk=nk, nslot=nslot, lam_init=lam_init)
    kblk = COL_DA_K // LANES
    return pl.pallas_call(
        body,
        grid=(batch, DA_HEADS, nk),
        in_specs=[
            pl.BlockSpec((8, LANES), lambda b, h, i: (0, 0)),
            pl.BlockSpec((None, LANES, tq), lambda b, h, i: (b, h, i)),
            pl.BlockSpec((seq, LANES), lambda b, h, i: (b, kblk + h)),
            pl.BlockSpec((None, None, DA_V_DIM + ONES_ROWS, seq), lambda b, h, i: (b, h, 0, 0)),
            pl.BlockSpec((kn.shape[0] // batch, 1, LANES), lambda b, h, i: (b, 0, 0)),
            pl.BlockSpec((None, 2, 2, LANES, tq), lambda b, h, i: (h, 0, 0, 0, 0)),
            pl.BlockSpec((2, tq, LANES), lambda b, h, i: (0, 0, 0)),
            pl.BlockSpec((None, 2, 1, tq), lambda b, h, i: (h, 0, 0, 0)),
            pl.BlockSpec((None, tq, tq), lambda b, h, i: (h, 0, 0)),
            pl.BlockSpec((DA_V_DIM, tq), lambda b, h, i: (0, 0)),
        ],
        out_specs=pl.BlockSpec((tq, LANES), lambda b, h, i: (b * nk + i, h)),
        out_shape=jax.ShapeDtypeStruct((t, DA_WIDTH), BF16),
        scratch_shapes=[
            pltpu.VMEM((4, LANES, tq), BF16),
        ] + [pltpu.VMEM((tq, tq), F32)] * (2 * nslot)
        + [pltpu.VMEM((tq, tq), BF16)] * (2 * nslot)
        + [pltpu.VMEM((1, tq), F32)] * (2 * nslot)
        + [pltpu.VMEM((1, tq), F32)] * (2 * nslot)
        + [pltpu.VMEM((DA_V_DIM + ONES_ROWS, tq), F32)] * 2
        + [pltpu.VMEM((1, tq), F32)] * 2,
        compiler_params=pltpu.CompilerParams(
            dimension_semantics=("arbitrary", "arbitrary", "arbitrary"),
            vmem_limit_bytes=VMEM_LIMIT),
        name="diff_attention",
    )(lamvec, q_t, proj, v_t, kn, augq, augk, qbias, corr, gnb)


def _log_sigmoid(z):
    return jnp.minimum(z, 0.0) - jnp.log1p(jnp.exp(-jnp.abs(z)))


def _gla_chunk(q, k, v, g, st_ref, reverse):
    c = GLA_CHUNK
    q = q.astype(F32)
    k = k.astype(F32)
    row = lax.broadcasted_iota(jnp.int32, (c, GLA_QK_WIDTH), 0)
    lane = lax.broadcasted_iota(jnp.int32, (c, GLA_QK_WIDTH), 1)
    b = g
    shift = 1
    while shift < c:
        b = b + jnp.where(row >= shift, pltpu.roll(b, shift, 0), 0.0)
        shift *= 2
    tot = b[c - 1:c, :]
    if reverse:
        b = tot - b + g
    q_t = (q * jnp.exp(b)).astype(BF16)
    k_t = (k * jnp.exp(-b)).astype(BF16)
    k_end = (k * jnp.exp(tot - b)).astype(BF16)
    decay = jnp.exp(tot)

    head = lane // GLA_DK
    zero_k = jnp.zeros_like(k_t)
    k_bd = jnp.concatenate([jnp.where(head == h, k_t, zero_k) for h in range(GLA_HEADS)], axis=0)
    att = _nt_dot(q_t, k_bd)
    pos = lane % c
    keep = (pos > row) if reverse else (pos <= row)
    att = jnp.where(keep, att, 0.0).astype(BF16)

    vhead = lax.broadcasted_iota(jnp.int32, (c, GLA_WIDTH), 1) // GLA_DV
    zero_v = jnp.zeros_like(v)
    v_bd = jnp.concatenate([jnp.where(vhead == h, v, zero_v) for h in range(GLA_HEADS)], axis=0)
    st = st_ref[...]
    st_b = st.astype(BF16)
    shead = lax.broadcasted_iota(jnp.int32, st.shape, 1) // GLA_DK
    zero_s = jnp.zeros_like(st_b)
    st_bd = jnp.concatenate([jnp.where(shead == h, st_b, zero_s) for h in range(GLA_HEADS)], axis=0)
    out = _dot(att, v_bd) + _nt_dot(q_t, st_bd)

    kv_t = _tn_dot(v, k_end)
    upd = kv_t[(GLA_HEADS - 1) * GLA_DV:]
    for h in range(GLA_HEADS - 2, -1, -1):
        upd = jnp.where(shead == h, kv_t[h * GLA_DV:(h + 1) * GLA_DV], upd)
    st_ref[...] = decay * st + upd
    return out


def _gla_body(qf_ref, kf_ref, vf_ref, lrf_ref, qb_ref, kb_ref, vb_ref, lrb_ref, wg_ref, bg_ref,
              of_ref, ob_ref, stf_ref, stb_ref, *, rows):
    @pl.when(pl.program_id(1) == 0)
    def _():
        stf_ref[...] = jnp.zeros(stf_ref.shape, F32)
        stb_ref[...] = jnp.zeros(stb_ref.shape, F32)

    wg = wg_ref[...]
    bg = bg_ref[...]
    zf = _dot(lrf_ref[...], wg[:, :GLA_QK_WIDTH]) + bg[:, :GLA_QK_WIDTH]
    zb = _dot(lrb_ref[...], wg[:, GLA_QK_WIDTH:]) + bg[:, GLA_QK_WIDTH:]
    gf = _log_sigmoid(zf) / GLA_GATE_NORM
    gb = _log_sigmoid(zb) / GLA_GATE_NORM
    nc = rows // GLA_CHUNK
    for ci in range(nc):
        sf = slice(ci * GLA_CHUNK, (ci + 1) * GLA_CHUNK)
        of_ref[sf, :] = _gla_chunk(qf_ref[sf, :], kf_ref[sf, :], vf_ref[sf, :], gf[sf, :],
                                   stf_ref, False)
        cj = nc - 1 - ci
        sb = slice(cj * GLA_CHUNK, (cj + 1) * GLA_CHUNK)
        ob_ref[sb, :] = _gla_chunk(qb_ref[sb, :], kb_ref[sb, :], vb_ref[sb, :], gb[sb, :],
                                   stb_ref, True)


def _gla(proj, lr, wg, bg, batch, seq, rows):
    t = proj.shape[0]
    nb = seq // rows
    qblk = COL_GQ // GLA_QK_WIDTH
    kblk = COL_GK // GLA_QK_WIDTH
    vblk = COL_GV // GLA_WIDTH

    def fwd(col):
        return lambda b, i: (b * nb + i, col)

    def bwd(col):
        return lambda b, i: (b * nb + nb - 1 - i, col)

    def specs(ix):
        return [
            pl.BlockSpec((rows, GLA_QK_WIDTH), ix(qblk)),
            pl.BlockSpec((rows, GLA_QK_WIDTH), ix(kblk)),
            pl.BlockSpec((rows, GLA_WIDTH), ix(vblk)),
            pl.BlockSpec((rows, LANES), ix(0)),
        ]

    return pl.pallas_call(
        functools.partial(_gla_body, rows=rows),
        grid=(batch, nb),
        in_specs=specs(fwd) + specs(bwd) + [
            pl.BlockSpec((LANES, 2 * GLA_QK_WIDTH), lambda b, i: (0, 0)),
            pl.BlockSpec((1, 2 * GLA_QK_WIDTH), lambda b, i: (0, 0)),
        ],
        out_specs=[
            pl.BlockSpec((rows, GLA_WIDTH), fwd(0)),
            pl.BlockSpec((rows, GLA_WIDTH), bwd(0)),
        ],
        out_shape=[jax.ShapeDtypeStruct((t, GLA_WIDTH), F32)] * 2,
        scratch_shapes=[pltpu.VMEM((GLA_DV, GLA_QK_WIDTH), F32)] * 2,
        compiler_params=pltpu.CompilerParams(
            dimension_semantics=("arbitrary", "arbitrary"), vmem_limit_bytes=VMEM_LIMIT),
        name="gla",
    )(proj, proj, proj, lr, proj, proj, proj, lr, wg, bg)


def _out_proj_body(da_ref, of_ref, ob_ref, gr_ref, x_ref, gng_ref, wout_ref, pmg_ref, pfg_ref,
                   x1_ref, h2_ref):
    o = of_ref[...] + ob_ref[...]
    gr = gr_ref[...].astype(F32)
    parts = []
    for h in range(GLA_HEADS):
        sl = slice(h * GLA_DV, (h + 1) * GLA_DV)
        gh = gr[:, sl]
        parts.append(_rms(o[:, sl], gng_ref[...]) * (gh * jax.nn.sigmoid(gh)))
    gla = jnp.concatenate(parts, axis=1).astype(BF16)
    mix = _dot(jnp.concatenate([da_ref[...], gla], axis=1), wout_ref[...])
    x1 = x_ref[...] + _rms(mix, pmg_ref[...])
    x1_ref[...] = x1
    h2_ref[...] = _rms(x1, pfg_ref[...]).astype(BF16)


def _out_proj(da, o_f, o_b, proj, x2d, gng, w_out, pmg, pfg, tm):
    t = x2d.shape[0]
    row = lambda i: (i, 0)
    const = lambda i: (0, 0)
    return pl.pallas_call(
        _out_proj_body,
        grid=(t // tm,),
        in_specs=[
            pl.BlockSpec((tm, DA_WIDTH), row),
            pl.BlockSpec((tm, GLA_WIDTH), row),
            pl.BlockSpec((tm, GLA_WIDTH), row),
            pl.BlockSpec((tm, GLA_WIDTH), lambda i: (i, COL_GR // GLA_WIDTH)),
            pl.BlockSpec((tm, D_MODEL), row),
            pl.BlockSpec((1, GLA_DV), const),
            pl.BlockSpec((D_MODEL, D_MODEL), const),
            pl.BlockSpec((1, D_MODEL), const),
            pl.BlockSpec((1, D_MODEL), const),
        ],
        out_specs=[pl.BlockSpec((tm, D_MODEL), row), pl.BlockSpec((tm, D_MODEL), row)],
        out_shape=[jax.ShapeDtypeStruct((t, D_MODEL), F32),
                   jax.ShapeDtypeStruct((t, D_MODEL), BF16)],
        compiler_params=pltpu.CompilerParams(
            dimension_semantics=("arbitrary",), vmem_limit_bytes=VMEM_LIMIT),
        name="out_proj",
    )(da, o_f, o_b, proj, x2d, gng, w_out, pmg, pfg)


HALO = 16


def _ffn_body(h_ref, hp_ref, hn_ref, wg_ref, wu_ref, cw_ref, cb_ref, wd_ref, x1_ref, p_ref,
              pfg_ref, wpg_ref, bpg_ref, wpp_ref, png_ref, o_ref, hext_ref, act_ref,
              *, tm, tf, seq, nff):
    i = pl.program_id(0)
    j = pl.program_id(1)

    @pl.when(j == 0)
    def _():
        first = (i * tm) % seq == 0
        last = ((i + 1) * tm) % seq == 0
        hp = hp_ref[...]
        hn = hn_ref[...]
        hext_ref[0:HALO, :] = jnp.where(first, jnp.zeros_like(hp), hp)
        hext_ref[HALO:HALO + tm, :] = h_ref[...]
        hext_ref[HALO + tm:, :] = jnp.where(last, jnp.zeros_like(hn), hn)

    gate = _dot(hext_ref[...], wg_ref[...])
    ext = tm + 2 * HALO
    cw = cw_ref[...]
    conv = (pltpu.roll(gate, 1, 0)[HALO:HALO + tm] * cw[0:1]
            + gate[HALO:HALO + tm] * cw[1:2]
            + pltpu.roll(gate, ext - 1, 0)[HALO:HALO + tm] * cw[2:3]
            + cb_ref[...])
    up = _dot(h_ref[...], wu_ref[...])
    act_ref[:, pl.ds(pl.multiple_of(j * tf, tf), tf)] = (jax.nn.gelu(conv) * up).astype(BF16)

    @pl.when(j == nff - 1)
    def _():
        x2 = x1_ref[...] + _rms(_dot(act_ref[...], wd_ref[...]), pfg_ref[...])
        e = _rms(_dot(p_ref[...].astype(BF16), wpp_ref[...]), png_ref[...])
        gate_e = jax.nn.sigmoid(_dot(x2.astype(BF16), wpg_ref[...]) + bpg_ref[...])
        o_ref[...] = x2 + gate_e * e


def _ffn(h2, x1, p2d, w_up, conv_w, conv_b, w_down, pfg, w_pg, b_pg, w_pp, png, seq, tm, tf):
    t = h2.shape[0]
    nff = D_FF // tf
    nhalo = t // HALO
    per = tm // HALO
    row = lambda i, j: (i, 0)
    const = lambda i, j: (0, 0)
    return pl.pallas_call(
        functools.partial(_ffn_body, tm=tm, tf=tf, seq=seq, nff=nff),
        grid=(t // tm, nff),
        in_specs=[
            pl.BlockSpec((tm, D_MODEL), row),
            pl.BlockSpec((HALO, D_MODEL), lambda i, j: (jnp.maximum(i * per - 1, 0), 0)),
            pl.BlockSpec((HALO, D_MODEL), lambda i, j: (jnp.minimum((i + 1) * per, nhalo - 1), 0)),
            pl.BlockSpec((D_MODEL, tf), lambda i, j: (0, j)),
            pl.BlockSpec((D_MODEL, tf), lambda i, j: (0, nff + j)),
            pl.BlockSpec((3, tf), lambda i, j: (0, j)),
            pl.BlockSpec((1, tf), lambda i, j: (0, j)),
            pl.BlockSpec((D_FF, D_MODEL), const, pipeline_mode=pl.Buffered(1)),
            pl.BlockSpec((tm, D_MODEL), row),
            pl.BlockSpec((tm, PLE_DIM), row),
            pl.BlockSpec((1, D_MODEL), const),
            pl.BlockSpec((D_MODEL, D_MODEL), const, pipeline_mode=pl.Buffered(1)),
            pl.BlockSpec((1, D_MODEL), const),
            pl.BlockSpec((PLE_DIM, D_MODEL), const, pipeline_mode=pl.Buffered(1)),
            pl.BlockSpec((1, D_MODEL), const),
        ],
        out_specs=pl.BlockSpec((tm, D_MODEL), row),
        out_shape=jax.ShapeDtypeStruct((t, D_MODEL), F32),
        scratch_shapes=[
            pltpu.VMEM((tm + 2 * HALO, D_MODEL), BF16),
            pltpu.VMEM((tm, D_FF), BF16),
        ],
        compiler_params=pltpu.CompilerParams(
            dimension_semantics=("arbitrary", "arbitrary"), vmem_limit_bytes=VMEM_LIMIT),
        name="conv_ffn",
    )(h2, h2, h2, w_up, w_up, conv_w, conv_b, w_down, x1, p2d, pfg, w_pg, b_pg, w_pp, png)


def _prep_weights(l, w_in, gla_w_gate_f, gla_b_gate_f, gla_w_gate_b, gla_b_gate_b, da_lq1,
                  da_lk1, da_lq2, da_lk2):
    w = w_in[l]
    qscale = jnp.concatenate([
        jnp.full((DA_WIDTH,), DA_QK_DIM ** -0.5 * LOG2E, F32), jnp.ones((2 * DA_WIDTH,), F32),
        jnp.full((GLA_QK_WIDTH,), GLA_DK ** -0.5, F32),
        jnp.ones((GLA_QK_WIDTH + 2 * GLA_WIDTH,), F32)])
    w_main = (w[:, :MAIN_WIDTH] * qscale).astype(BF16)
    w_lr = jnp.zeros((D_MODEL, LANES), F32).at[:, :2 * GLA_GATE_RANK].set(w[:, MAIN_WIDTH:])
    wg = jnp.zeros((LANES, 2 * GLA_QK_WIDTH), F32)
    wg = wg.at[:GLA_GATE_RANK, :GLA_QK_WIDTH].set(gla_w_gate_f[l])
    wg = wg.at[GLA_GATE_RANK:2 * GLA_GATE_RANK, GLA_QK_WIDTH:].set(gla_w_gate_b[l])
    bg = jnp.concatenate([gla_b_gate_f[l], gla_b_gate_b[l]])[None, :]
    lamvec = jnp.zeros((8, LANES), F32)
    for r, vec in enumerate((da_lq1, da_lk1, da_lq2, da_lk2)):
        lamvec = lamvec.at[r, :DA_QK_DIM].set(vec[l].astype(F32))
    return w_main, w_lr.astype(BF16), wg.astype(BF16), bg, lamvec


ROW_TILE = 512
ATTN_TILE = 1024
GLA_ROWS = 512
FF_TILE = 2048


def _attn_slots(seq, tile):
    return min(NSLOT, max(2, seq // tile))


def _attn_tile(seq):
    tile = min(seq, ATTN_TILE)
    while tile > LANES:
        resident = 2 * 2 * seq * (LANES + DA_V_DIM + ONES_ROWS)
        tiles = tile * tile * (2 * _attn_slots(seq, tile) * (4 + 2) + 2 * 4)
        if resident + tiles <= VMEM_LIMIT * 3 // 4:
            break
        tile //= 2
    return tile


def _layer(x2d, p2d, batch, seq, lam_init, wts):
    tm = min(seq, ROW_TILE)
    proj, q_t, v_t, lr, kn = _in_proj(x2d, wts["pre_mix_g"], wts["w_main"], wts["w_lr"], batch, seq, tm)
    da = _diff_attention(proj, q_t, v_t, kn, wts["lamvec"], wts["da_norm_g"], batch, seq,
                         _attn_tile(seq), lam_init)
    o_f, o_b = _gla(proj, lr, wts["wg"], wts["bg"], batch, seq, min(seq, GLA_ROWS))
    x1, h2 = _out_proj(da, o_f, o_b, proj, x2d, wts["gla_norm_g"], wts["w_out"],
                       wts["post_mix_g"], wts["pre_ffn_g"], tm)
    return _ffn(h2, x1, p2d, wts["w_ffn_up"], wts["ffn_conv_w"], wts["ffn_conv_b"],
                wts["w_ffn_down"], wts["post_ffn_g"], wts["w_ple_gate"], wts["b_ple_gate"],
                wts["w_ple_proj"], wts["ple_norm_g"], seq, tm, FF_TILE)


def kernel(x_prompt, x_sample, p_prompt, p_sample, pre_mix_g, w_in, da_lq1, da_lk1, da_lq2, da_lk2, da_norm_g, gla_w_gate_f, gla_b_gate_f, gla_w_gate_b, gla_b_gate_b, gla_norm_g, w_out, post_mix_g, pre_ffn_g, w_ffn_up, ffn_conv_w, ffn_conv_b, w_ffn_down, post_ffn_g, w_ple_gate, b_ple_gate, w_ple_proj, ple_norm_g):
    depth = w_in.shape[0]
    layers = []
    for l in range(depth):
        w_main, w_lr, wg, bg, lamvec = _prep_weights(
            l, w_in, gla_w_gate_f, gla_b_gate_f, gla_w_gate_b, gla_b_gate_b,
            da_lq1, da_lk1, da_lq2, da_lk2)
        row = lambda a: a[l][None, :].astype(F32)
        layers.append(dict(
            w_main=w_main, w_lr=w_lr, wg=wg, bg=bg, lamvec=lamvec,
            pre_mix_g=row(pre_mix_g), da_norm_g=row(da_norm_g), gla_norm_g=row(gla_norm_g),
            w_out=w_out[l].astype(BF16), post_mix_g=row(post_mix_g), pre_ffn_g=row(pre_ffn_g),
            w_ffn_up=w_ffn_up[l].astype(BF16), ffn_conv_w=ffn_conv_w[l].astype(F32),
            ffn_conv_b=row(ffn_conv_b), w_ffn_down=w_ffn_down[l].astype(BF16),
            post_ffn_g=row(post_ffn_g), w_ple_gate=w_ple_gate[l].astype(BF16),
            b_ple_gate=row(b_ple_gate), w_ple_proj=w_ple_proj[l].astype(BF16),
            ple_norm_g=row(ple_norm_g)))

    def trunk(x, p):
        batch, seq, _ = x.shape
        t = batch * seq
        x2d = x.reshape(t, D_MODEL)
        for l in range(depth):
            lam_init = 0.8 - 0.6 * math.exp(-0.3 * l)
            x2d = _layer(x2d, p[l].reshape(t, PLE_DIM), batch, seq, lam_init, layers[l])
        return x2d.reshape(batch, seq, D_MODEL)

    return (trunk(x_prompt, p_prompt), trunk(x_sample, p_sample))
```

```python
import functools
import math

import jax
import jax.numpy as jnp
from jax import lax
from jax.experimental import pallas as pl
from jax.experimental.pallas import tpu as pltpu

F32 = jnp.float32
BF16 = jnp.bfloat16

D_MODEL = 1024
PLE_DIM = 256
DA_HEADS = 4
DA_QK_DIM = 64
DA_V_DIM = 128
DA_WIDTH = DA_HEADS * DA_V_DIM
GLA_HEADS = 4
GLA_DK = 64
GLA_DV = 128
GLA_QK_WIDTH = GLA_HEADS * GLA_DK
GLA_WIDTH = GLA_HEADS * GLA_DV
GLA_GATE_RANK = 16
GLA_GATE_NORM = 16.0
GLA_CHUNK = 64
D_FF = 4 * D_MODEL
NORM_EPS = 1e-6
LANES = 128
MAIN_WIDTH = 3 * DA_WIDTH + 2 * GLA_QK_WIDTH + 2 * GLA_WIDTH
NEG_BIG = -1e30
LOG2E = math.log2(math.e)
ZERO_PROB_GAP = 150.0
ONES_ROWS = 16
NSLOT = 2
VMEM_LIMIT = 56 * 1024 * 1024

COL_DA_Q = 0
COL_DA_K = DA_WIDTH
COL_DA_V = 2 * DA_WIDTH
COL_GLA = 3 * DA_WIDTH
PROJ_WIDTH = DA_WIDTH + 2 * GLA_QK_WIDTH + 2 * GLA_WIDTH
PCOL_DA_K = 0
PCOL_GQ = DA_WIDTH
PCOL_GK = PCOL_GQ + GLA_QK_WIDTH
PCOL_GV = PCOL_GK + GLA_QK_WIDTH
PCOL_GR = PCOL_GV + GLA_WIDTH


def _rms(x, g):
    return x * lax.rsqrt(jnp.mean(x * x, axis=-1, keepdims=True) + NORM_EPS) * g


def _nt_dot(a, b):
    return lax.dot_general(a, b, (((1,), (1,)), ((), ())), preferred_element_type=F32)


def _tn_dot(a, b):
    return lax.dot_general(a, b, (((0,), (0,)), ((), ())), preferred_element_type=F32)


def _dot(a, b):
    return jnp.dot(a, b, preferred_element_type=F32)


def _in_proj_body(x_ref, g_ref, wm_ref, wlr_ref, gsel_ref, proj_ref, qt_ref, vt_ref, lr_ref, kn_ref):
    h = _rms(x_ref[...], g_ref[...]).astype(BF16)
    main = _dot(h, wm_ref[...])
    proj_ref[:, :DA_WIDTH] = main[:, COL_DA_K:COL_DA_K + DA_WIDTH].astype(BF16)
    proj_ref[:, DA_WIDTH:] = main[:, COL_GLA:].astype(BF16)
    qt_ref[...] = main[:, COL_DA_Q:COL_DA_Q + DA_WIDTH].T.astype(BF16)
    for hd in range(DA_HEADS):
        c0 = COL_DA_V + hd * DA_V_DIM
        vt_ref[hd, :DA_V_DIM, :] = main[:, c0:c0 + DA_V_DIM].T.astype(BF16)
        vt_ref[hd, DA_V_DIM:, :] = jnp.ones((ONES_ROWS, main.shape[0]), BF16)
    lr_ref[...] = _dot(h, wlr_ref[...]).astype(BF16)
    k = main[:, COL_DA_K:COL_DA_K + DA_WIDTH]
    kn_ref[...] = jnp.max(_dot((k * k).astype(BF16), gsel_ref[...]), axis=0, keepdims=True)


def _in_proj(x2d, g, w_main, w_lr, batch, seq, tm):
    t = x2d.shape[0]
    per_seq = seq // tm
    va_rows = DA_V_DIM + ONES_ROWS
    col_group = jnp.arange(DA_WIDTH, dtype=jnp.int32)[:, None] // DA_QK_DIM
    gsel = (col_group == jnp.arange(LANES, dtype=jnp.int32)[None, :]).astype(BF16)
    return pl.pallas_call(
        _in_proj_body,
        grid=(t // tm,),
        in_specs=[
            pl.BlockSpec((tm, D_MODEL), lambda i: (i, 0)),
            pl.BlockSpec((1, D_MODEL), lambda i: (0, 0)),
            pl.BlockSpec((D_MODEL, MAIN_WIDTH), lambda i: (0, 0)),
            pl.BlockSpec((D_MODEL, LANES), lambda i: (0, 0)),
            pl.BlockSpec((DA_WIDTH, LANES), lambda i: (0, 0)),
        ],
        out_specs=[
            pl.BlockSpec((tm, PROJ_WIDTH), lambda i: (i, 0)),
            pl.BlockSpec((None, DA_WIDTH, tm), lambda i: (i // per_seq, 0, i % per_seq)),
            pl.BlockSpec((None, DA_HEADS, va_rows, tm),
                         lambda i: (i // per_seq, 0, 0, i % per_seq)),
            pl.BlockSpec((tm, LANES), lambda i: (i, 0)),
            pl.BlockSpec((None, 1, LANES), lambda i: (i, 0, 0)),
        ],
        out_shape=[
            jax.ShapeDtypeStruct((t, PROJ_WIDTH), BF16),
            jax.ShapeDtypeStruct((batch, DA_WIDTH, seq), BF16),
            jax.ShapeDtypeStruct((batch, DA_HEADS, va_rows, seq), BF16),
            jax.ShapeDtypeStruct((t, LANES), BF16),
            jax.ShapeDtypeStruct((t // tm, 1, LANES), F32),
        ],
        compiler_params=pltpu.CompilerParams(
            dimension_semantics=("arbitrary",), vmem_limit_bytes=VMEM_LIMIT),
        name="in_proj",
    )(x2d, g, w_main, w_lr, gsel)


def _split3(x):
    hi = x.astype(BF16).astype(F32)
    mid = (x - hi).astype(BF16).astype(F32)
    lo = (x - hi - mid).astype(BF16).astype(F32)
    return hi, mid, lo


def _attn_consts(tq):
    i = jnp.arange(tq, dtype=jnp.int32)
    lo = (i & 255).astype(F32)
    hi = (i - (i & 255)).astype(F32)
    slopes = jnp.asarray([2.0 ** (-8.0 * (h + 1) / DA_HEADS) for h in range(DA_HEADS)], F32) * LOG2E
    pieces = jnp.stack([p for piece in _split3(slopes) for p in (piece, piece)], axis=1)
    half = jnp.zeros((DA_HEADS, DA_QK_DIM, tq), F32).at[:, :6].set(
        jnp.broadcast_to(pieces[:, :, None], (DA_HEADS, 6, tq)))
    zeros = jnp.zeros_like(half)
    left = jnp.stack([jnp.concatenate([zeros, half], 1),
                      jnp.concatenate([half, zeros], 1)], axis=1)
    augq = jnp.stack([left, -left], axis=1).astype(BF16)
    kcols = jnp.stack([lo, hi, lo, hi, lo, hi], axis=-1)
    khalf = jnp.zeros((tq, DA_QK_DIM), F32).at[:, :6].set(kcols)
    kz = jnp.zeros_like(khalf)
    augk = jnp.stack([jnp.concatenate([kz, khalf], -1),
                      jnp.concatenate([khalf, kz], -1)], axis=0).astype(BF16)
    qb = slopes[:, None] * i.astype(F32)[None, :]
    qbias = jnp.stack([-qb, qb], axis=1)[:, :, None, :]
    d = (i[None, :] - i[:, None]).astype(F32)
    corr = 2.0 * slopes[:, None, None] * jnp.minimum(d, 0.0)[None]
    return augq, augk, qbias, corr


def _attn_body(lam_ref, qt_ref, k_ref, vt_ref, kn_ref, augq_ref, augk_ref, qbias_ref, corr_ref, gn_ref,
               o_ref, qa_ref, *scratch, tq, nk, nslot, lam_init):
    grab = lambda g: [scratch[2 * (g * nslot + sl):2 * (g * nslot + sl) + 2] for sl in range(nslot)]
    s_ref, p_ref, cmax_ref, alpha_ref = grab(0), grab(1), grab(2), grab(3)
    acc_ref, m_ref = scratch[8 * nslot:8 * nslot + 2], scratch[8 * nslot + 2:8 * nslot + 4]
    h = pl.program_id(1)
    qi = pl.program_id(2)
    slope = jnp.where(h == 0, 2.0 ** -2, jnp.where(h == 1, 2.0 ** -4,
                      jnp.where(h == 2, 2.0 ** -6, 2.0 ** -8))).astype(F32) * LOG2E

    qt = qt_ref[...]
    row = lax.broadcasted_iota(jnp.int32, (LANES, tq), 0)
    for side in range(2):
        qa_ref[2 * side] = jnp.where(row < DA_QK_DIM, qt, augq_ref[side, 0])
        qa_ref[2 * side + 1] = jnp.where(row >= DA_QK_DIM, qt, augq_ref[side, 1])
    for mp in range(2):
        m_ref[mp][...] = jnp.full(m_ref[mp].shape, NEG_BIG, F32)
        acc_ref[mp][...] = jnp.zeros(acc_ref[mp].shape, F32)
    lane = lax.broadcasted_iota(jnp.int32, (tq, LANES), 1)

    def scores(kt, side, slot, diag):
        k = k_ref[pl.ds(pl.multiple_of(kt * tq, tq), tq), :]
        ka = (jnp.where(lane < DA_QK_DIM, k, augk_ref[0]),
              jnp.where(lane >= DA_QK_DIM, k, augk_ref[1]))
        for mp in range(2):
            s = _dot(ka[mp], qa_ref[2 * side + mp])
            if diag:
                s = s + corr_ref[...]
            s_ref[slot][mp][...] = s
            cmax_ref[slot][mp][...] = jnp.max(s, axis=0, keepdims=True) + qbias_ref[side]

    scores(qi, 0, 0, True)

    qf = qt.astype(F32)
    lane1 = lax.broadcasted_iota(jnp.int32, (1, LANES), 1)
    kn2 = jnp.max(kn_ref[...], axis=0)
    gap = None
    for mp in range(2):
        rows = slice(mp * DA_QK_DIM, (mp + 1) * DA_QK_DIM)
        qn2 = jnp.max(jnp.sum(qf[rows] * qf[rows], axis=0, keepdims=True), axis=1, keepdims=True)
        kn2_mp = jnp.max(jnp.where(lane1 == 2 * h + mp, kn2, 0.0), axis=1, keepdims=True)
        m_min = jnp.min(cmax_ref[0][mp][...], axis=1, keepdims=True)
        g = 1.02 * jnp.sqrt(qn2 * kn2_mp) - m_min
        gap = g if gap is None else jnp.maximum(gap, g)
    far = ((gap + ZERO_PROB_GAP) / slope - 1.0) / tq
    reach = jnp.where(far < nk, jnp.maximum(jnp.ceil(far), 0.0), float(nk)).astype(jnp.int32)[0, 0]
    n_left = jnp.minimum(qi, reach)
    n_tiles = n_left + jnp.minimum(nk - 1 - qi, reach)

    def key_tile(t):
        return jnp.where(t <= n_left, qi - t, qi + (t - n_left))

    def tile_side(t):
        return (t > n_left).astype(jnp.int32)

    def tile_const(t):
        return -slope * (jnp.abs(qi - key_tile(t)) * tq).astype(F32)

    def probs(t, slot):
        shift = tile_const(t) + qbias_ref[tile_side(t)]
        for mp in range(2):
            m_old = m_ref[mp][...]
            m_new = jnp.maximum(m_old, cmax_ref[slot][mp][...] + tile_const(t))
            alpha_ref[slot][mp][...] = jnp.exp2(m_old - m_new)
            p_ref[slot][mp][...] = jnp.exp2(s_ref[slot][mp][...] - (m_new - shift)).astype(BF16)
            m_ref[mp][...] = m_new

    def accumulate(t, slot):
        va = vt_ref[:, pl.ds(pl.multiple_of(key_tile(t) * tq, tq), tq)]
        for mp in range(2):
            acc_ref[mp][...] = (alpha_ref[slot][mp][...] * acc_ref[mp][...]
                                + _dot(va, p_ref[slot][mp][...]))

    def step(t, slot):
        prev = (slot - 1) % nslot

        @pl.when(t <= n_tiles)
        def _():
            probs(t - 1, prev)
            scores(key_tile(t), tile_side(t), slot, False)
            accumulate(t - 1, prev)

    def trip(i, carry):
        for r in range(1, nslot + 1):
            step(nslot * i + r, r % nslot)
        return carry

    lax.fori_loop(0, (n_tiles + nslot - 1) // nslot, trip, 0)
    for r in range(nslot):
        @pl.when(n_tiles % nslot == r)
        def _(r=r):
            probs(n_tiles, r)
            accumulate(n_tiles, r)

    lv = lam_ref[...]
    lam = (jnp.exp(jnp.sum(lv[0:1] * lv[1:2], axis=-1, keepdims=True))
           - jnp.exp(jnp.sum(lv[2:3] * lv[3:4], axis=-1, keepdims=True)) + lam_init)
    a0 = acc_ref[0][...]
    a1 = acc_ref[1][...]
    o0 = a0[:DA_V_DIM] * (1.0 / a0[DA_V_DIM:DA_V_DIM + 1])
    o1 = a1[:DA_V_DIM] * (1.0 / a1[DA_V_DIM:DA_V_DIM + 1])
    out = o0 - lam * o1
    inv = lax.rsqrt(jnp.mean(out * out, axis=0, keepdims=True) + NORM_EPS)
    out = out * inv * gn_ref[...] * (1.0 - lam_init)
    o_ref[...] = out.T.astype(o_ref.dtype)


def _diff_attention(proj, q_t, v_t, kn, lamvec, gn, batch, seq, tq, lam_init):
    t = proj.shape[0]
    nk = seq // tq
    augq, augk, qbias, corr = _attn_consts(tq)
    gnb = jnp.broadcast_to(gn.reshape(DA_V_DIM, 1), (DA_V_DIM, tq))
    nslot = _attn_slots(seq, tq)
    body = functools.partial(_attn_body, tq=tq, nk=nk, nslot=nslot, lam_init=lam_init)
    kblk = PCOL_DA_K // LANES
    return pl.pallas_call(
        body,
        grid=(batch, DA_HEADS, nk),
        in_specs=[
            pl.BlockSpec((8, LANES), lambda b, h, i: (0, 0)),
            pl.BlockSpec((None, LANES, tq), lambda b, h, i: (b, h, i)),
            pl.BlockSpec((seq, LANES), lambda b, h, i: (b, kblk + h)),
            pl.BlockSpec((None, None, DA_V_DIM + ONES_ROWS, seq), lambda b, h, i: (b, h, 0, 0)),
            pl.BlockSpec((kn.shape[0] // batch, 1, LANES), lambda b, h, i: (b, 0, 0)),
            pl.BlockSpec((None, 2, 2, LANES, tq), lambda b, h, i: (h, 0, 0, 0, 0)),
            pl.BlockSpec((2, tq, LANES), lambda b, h, i: (0, 0, 0)),
            pl.BlockSpec((None, 2, 1, tq), lambda b, h, i: (h, 0, 0, 0)),
            pl.BlockSpec((None, tq, tq), lambda b, h, i: (h, 0, 0)),
            pl.BlockSpec((DA_V_DIM, tq), lambda b, h, i: (0, 0)),
        ],
        out_specs=pl.BlockSpec((tq, LANES), lambda b, h, i: (b * nk + i, h)),
        out_shape=jax.ShapeDtypeStruct((t, DA_WIDTH), BF16),
        scratch_shapes=[
            pltpu.VMEM((4, LANES, tq), BF16),
        ] + [pltpu.VMEM((tq, tq), F32)] * (2 * nslot)
        + [pltpu.VMEM((tq, tq), BF16)] * (2 * nslot)
        + [pltpu.VMEM((1, tq), F32)] * (2 * nslot)
        + [pltpu.VMEM((1, tq), F32)] * (2 * nslot)
        + [pltpu.VMEM((DA_V_DIM + ONES_ROWS, tq), F32)] * 2
        + [pltpu.VMEM((1, tq), F32)] * 2,
        compiler_params=pltpu.CompilerParams(
            dimension_semantics=("arbitrary", "arbitrary", "arbitrary"),
            vmem_limit_bytes=VMEM_LIMIT),
        name="diff_attention",
    )(lamvec, q_t, proj, v_t, kn, augq, augk, qbias, corr, gnb)


def _log_sigmoid(z):
    return jnp.minimum(z, 0.0) - jnp.log1p(jnp.exp(-jnp.abs(z)))


def _gla_chunk(q, k, v, g, st_ref, reverse):
    c = GLA_CHUNK
    q = q.astype(F32)
    k = k.astype(F32)
    row = lax.broadcasted_iota(jnp.int32, (c, GLA_QK_WIDTH), 0)
    lane = lax.broadcasted_iota(jnp.int32, (c, GLA_QK_WIDTH), 1)
    b = g
    shift = 1
    while shift < c:
        b = b + jnp.where(row >= shift, pltpu.roll(b, shift, 0), 0.0)
        shift *= 2
    tot = b[c - 1:c, :]
    if reverse:
        b = tot - b + g
    q_t = (q * jnp.exp(b)).astype(BF16)
    k_t = (k * jnp.exp(-b)).astype(BF16)
    k_end = (k * jnp.exp(tot - b)).astype(BF16)
    decay = jnp.exp(tot)

    head = lane // GLA_DK
    zero_k = jnp.zeros_like(k_t)
    k_bd = jnp.concatenate([jnp.where(head == h, k_t, zero_k) for h in range(GLA_HEADS)], axis=0)
    att = _nt_dot(q_t, k_bd)
    pos = lane % c
    keep = (pos > row) if reverse else (pos <= row)
    att = jnp.where(keep, att, 0.0).astype(BF16)

    vhead = lax.broadcasted_iota(jnp.int32, (c, GLA_WIDTH), 1) // GLA_DV
    zero_v = jnp.zeros_like(v)
    v_bd = jnp.concatenate([jnp.where(vhead == h, v, zero_v) for h in range(GLA_HEADS)], axis=0)
    st = st_ref[...]
    st_b = st.astype(BF16)
    shead = lax.broadcasted_iota(jnp.int32, st.shape, 1) // GLA_DK
    zero_s = jnp.zeros_like(st_b)
    st_bd = jnp.concatenate([jnp.where(shead == h, st_b, zero_s) for h in range(GLA_HEADS)], axis=0)
    out = _dot(att, v_bd) + _nt_dot(q_t, st_bd)

    kv_t = _tn_dot(v, k_end)
    upd = kv_t[(GLA_HEADS - 1) * GLA_DV:]
    for h in range(GLA_HEADS - 2, -1, -1):
        upd = jnp.where(shead == h, kv_t[h * GLA_DV:(h + 1) * GLA_DV], upd)
    st_ref[...] = decay * st + upd
    return out


def _gla_body(qf_ref, kf_ref, vf_ref, lrf_ref, qb_ref, kb_ref, vb_ref, lrb_ref, wg_ref, bg_ref,
              of_ref, ob_ref, stf_ref, stb_ref, *, rows):
    @pl.when(pl.program_id(1) == 0)
    def _():
        stf_ref[...] = jnp.zeros(stf_ref.shape, F32)
        stb_ref[...] = jnp.zeros(stb_ref.shape, F32)

    wg = wg_ref[...]
    bg = bg_ref[...]
    zf = _dot(lrf_ref[...], wg[:, :GLA_QK_WIDTH]) + bg[:, :GLA_QK_WIDTH]
    zb = _dot(lrb_ref[...], wg[:, GLA_QK_WIDTH:]) + bg[:, GLA_QK_WIDTH:]
    gf = _log_sigmoid(zf) / GLA_GATE_NORM
    gb = _log_sigmoid(zb) / GLA_GATE_NORM
    nc = rows // GLA_CHUNK
    for ci in range(nc):
        sf = slice(ci * GLA_CHUNK, (ci + 1) * GLA_CHUNK)
        of_ref[sf, :] = _gla_chunk(qf_ref[sf, :], kf_ref[sf, :], vf_ref[sf, :], gf[sf, :],
                                   stf_ref, False).astype(of_ref.dtype)
        cj = nc - 1 - ci
        sb = slice(cj * GLA_CHUNK, (cj + 1) * GLA_CHUNK)
        ob_ref[sb, :] = _gla_chunk(qb_ref[sb, :], kb_ref[sb, :], vb_ref[sb, :], gb[sb, :],
                                   stb_ref, True).astype(ob_ref.dtype)


def _gla(proj, lr, wg, bg, batch, seq, rows):
    t = proj.shape[0]
    nb = seq // rows
    qblk = PCOL_GQ // GLA_QK_WIDTH
    kblk = PCOL_GK // GLA_QK_WIDTH
    vblk = PCOL_GV // GLA_WIDTH

    def fwd(col):
        return lambda b, i: (b * nb + i, col)

    def bwd(col):
        return lambda b, i: (b * nb + nb - 1 - i, col)

    def specs(ix):
        return [
            pl.BlockSpec((rows, GLA_QK_WIDTH), ix(qblk)),
            pl.BlockSpec((rows, GLA_QK_WIDTH), ix(kblk)),
            pl.BlockSpec((rows, GLA_WIDTH), ix(vblk)),
            pl.BlockSpec((rows, LANES), ix(0)),
        ]

    return pl.pallas_call(
        functools.partial(_gla_body, rows=rows),
        grid=(batch, nb),
        in_specs=specs(fwd) + specs(bwd) + [
            pl.BlockSpec((LANES, 2 * GLA_QK_WIDTH), lambda b, i: (0, 0)),
            pl.BlockSpec((1, 2 * GLA_QK_WIDTH), lambda b, i: (0, 0)),
        ],
        out_specs=[
            pl.BlockSpec((rows, GLA_WIDTH), fwd(0)),
            pl.BlockSpec((rows, GLA_WIDTH), bwd(0)),
        ],
        out_shape=[jax.ShapeDtypeStruct((t, GLA_WIDTH), BF16)] * 2,
        scratch_shapes=[pltpu.VMEM((GLA_DV, GLA_QK_WIDTH), F32)] * 2,
        compiler_params=pltpu.CompilerParams(
            dimension_semantics=("arbitrary", "arbitrary"), vmem_limit_bytes=VMEM_LIMIT),
        name="gla",
    )(proj, proj, proj, lr, proj, proj, proj, lr, wg, bg)


def _out_proj_body(da_ref, of_ref, ob_ref, gr_ref, x_ref, gng_ref, wout_ref, pmg_ref, pfg_ref,
                   x1_ref, h2_ref):
    o = of_ref[...].astype(F32) + ob_ref[...].astype(F32)
    gr = gr_ref[...].astype(F32)
    parts = []
    for h in range(GLA_HEADS):
        sl = slice(h * GLA_DV, (h + 1) * GLA_DV)
        gh = gr[:, sl]
        parts.append(_rms(o[:, sl], gng_ref[...]) * (gh * jax.nn.sigmoid(gh)))
    gla = jnp.concatenate(parts, axis=1).astype(BF16)
    mix = _dot(jnp.concatenate([da_ref[...], gla], axis=1), wout_ref[...])
    x1 = x_ref[...] + _rms(mix, pmg_ref[...])
    x1_ref[...] = x1
    h2_ref[...] = _rms(x1, pfg_ref[...]).astype(BF16)


def _out_proj(da, o_f, o_b, proj, x2d, gng, w_out, pmg, pfg, tm):
    t = x2d.shape[0]
    row = lambda i: (i, 0)
    const = lambda i: (0, 0)
    return pl.pallas_call(
        _out_proj_body,
        grid=(t // tm,),
        in_specs=[
            pl.BlockSpec((tm, DA_WIDTH), row),
            pl.BlockSpec((tm, GLA_WIDTH), row),
            pl.BlockSpec((tm, GLA_WIDTH), row),
            pl.BlockSpec((tm, GLA_WIDTH), lambda i: (i, PCOL_GR // GLA_WIDTH)),
            pl.BlockSpec((tm, D_MODEL), row),
            pl.BlockSpec((1, GLA_DV), const),
            pl.BlockSpec((D_MODEL, D_MODEL), const),
            pl.BlockSpec((1, D_MODEL), const),
            pl.BlockSpec((1, D_MODEL), const),
        ],
        out_specs=[pl.BlockSpec((tm, D_MODEL), row), pl.BlockSpec((tm, D_MODEL), row)],
        out_shape=[jax.ShapeDtypeStruct((t, D_MODEL), F32),
                   jax.ShapeDtypeStruct((t, D_MODEL), BF16)],
        compiler_params=pltpu.CompilerParams(
            dimension_semantics=("arbitrary",), vmem_limit_bytes=VMEM_LIMIT),
        name="out_proj",
    )(da, o_f, o_b, proj, x2d, gng, w_out, pmg, pfg)


HALO = 16


def _ffn_body(h_ref, hp_ref, hn_ref, wg_ref, wu_ref, cw_ref, cb_ref, wd_ref, x1_ref, p_ref,
              pfg_ref, wpg_ref, bpg_ref, wpp_ref, png_ref, o_ref, hext_ref, act_ref,
              *, tm, tf, seq, nff):
    i = pl.program_id(0)
    j = pl.program_id(1)

    @pl.when(j == 0)
    def _():
        first = (i * tm) % seq == 0
        last = ((i + 1) * tm) % seq == 0
        hp = hp_ref[...]
        hn = hn_ref[...]
        hext_ref[0:HALO, :] = jnp.where(first, jnp.zeros_like(hp), hp)
        hext_ref[HALO:HALO + tm, :] = h_ref[...]
        hext_ref[HALO + tm:, :] = jnp.where(last, jnp.zeros_like(hn), hn)

    gate = _dot(hext_ref[...], wg_ref[...])
    ext = tm + 2 * HALO
    cw = cw_ref[...]
    conv = (pltpu.roll(gate, 1, 0)[HALO:HALO + tm] * cw[0:1]
            + gate[HALO:HALO + tm] * cw[1:2]
            + pltpu.roll(gate, ext - 1, 0)[HALO:HALO + tm] * cw[2:3]
            + cb_ref[...])
    up = _dot(h_ref[...], wu_ref[...])
    act_ref[:, pl.ds(pl.multiple_of(j * tf, tf), tf)] = (jax.nn.gelu(conv) * up).astype(BF16)

    @pl.when(j == nff - 1)
    def _():
        x2 = x1_ref[...] + _rms(_dot(act_ref[...], wd_ref[...]), pfg_ref[...])
        e = _rms(_dot(p_ref[...].astype(BF16), wpp_ref[...]), png_ref[...])
        gate_e = jax.nn.sigmoid(_dot(x2.astype(BF16), wpg_ref[...]) + bpg_ref[...])
        o_ref[...] = x2 + gate_e * e


def _ffn(h2, x1, p2d, w_up, conv_w, conv_b, w_down, pfg, w_pg, b_pg, w_pp, png, seq, tm, tf):
    t = h2.shape[0]
    nff = D_FF // tf
    nhalo = t // HALO
    per = tm // HALO
    row = lambda i, j: (i, 0)
    const = lambda i, j: (0, 0)
    return pl.pallas_call(
        functools.partial(_ffn_body, tm=tm, tf=tf, seq=seq, nff=nff),
        grid=(t // tm, nff),
        in_specs=[
            pl.BlockSpec((tm, D_MODEL), row),
            pl.BlockSpec((HALO, D_MODEL), lambda i, j: (jnp.maximum(i * per - 1, 0), 0)),
            pl.BlockSpec((HALO, D_MODEL), lambda i, j: (jnp.minimum((i + 1) * per, nhalo - 1), 0)),
            pl.BlockSpec((D_MODEL, tf), lambda i, j: (0, j)),
            pl.BlockSpec((D_MODEL, tf), lambda i, j: (0, nff + j)),
            pl.BlockSpec((3, tf), lambda i, j: (0, j)),
            pl.BlockSpec((1, tf), lambda i, j: (0, j)),
            pl.BlockSpec((D_FF, D_MODEL), const, pipeline_mode=pl.Buffered(1)),
            pl.BlockSpec((tm, D_MODEL), row),
            pl.BlockSpec((tm, PLE_DIM), row),
            pl.BlockSpec((1, D_MODEL), const),
            pl.BlockSpec((D_MODEL, D_MODEL), const, pipeline_mode=pl.Buffered(1)),
            pl.BlockSpec((1, D_MODEL), const),
            pl.BlockSpec((PLE_DIM, D_MODEL), const, pipeline_mode=pl.Buffered(1)),
            pl.BlockSpec((1, D_MODEL), const),
        ],
        out_specs=pl.BlockSpec((tm, D_MODEL), row),
        out_shape=jax.ShapeDtypeStruct((t, D_MODEL), F32),
        scratch_shapes=[
            pltpu.VMEM((tm + 2 * HALO, D_MODEL), BF16),
            pltpu.VMEM((tm, D_FF), BF16),
        ],
        compiler_params=pltpu.CompilerParams(
            dimension_semantics=("arbitrary", "arbitrary"), vmem_limit_bytes=VMEM_LIMIT),
        name="conv_ffn",
    )(h2, h2, h2, w_up, w_up, conv_w, conv_b, w_down, x1, p2d, pfg, w_pg, b_pg, w_pp, png)


def _prep_weights(l, w_in, gla_w_gate_f, gla_b_gate_f, gla_w_gate_b, gla_b_gate_b, da_lq1,
                  da_lk1, da_lq2, da_lk2):
    w = w_in[l]
    qscale = jnp.concatenate([
        jnp.full((DA_WIDTH,), DA_QK_DIM ** -0.5 * LOG2E, F32), jnp.ones((2 * DA_WIDTH,), F32),
        jnp.full((GLA_QK_WIDTH,), GLA_DK ** -0.5, F32),
        jnp.ones((GLA_QK_WIDTH + 2 * GLA_WIDTH,), F32)])
    w_main = (w[:, :MAIN_WIDTH] * qscale).astype(BF16)
    w_lr = jnp.zeros((D_MODEL, LANES), F32).at[:, :2 * GLA_GATE_RANK].set(w[:, MAIN_WIDTH:])
    wg = jnp.zeros((LANES, 2 * GLA_QK_WIDTH), F32)
    wg = wg.at[:GLA_GATE_RANK, :GLA_QK_WIDTH].set(gla_w_gate_f[l])
    wg = wg.at[GLA_GATE_RANK:2 * GLA_GATE_RANK, GLA_QK_WIDTH:].set(gla_w_gate_b[l])
    bg = jnp.concatenate([gla_b_gate_f[l], gla_b_gate_b[l]])[None, :]
    lamvec = jnp.zeros((8, LANES), F32)
    for r, vec in enumerate((da_lq1, da_lk1, da_lq2, da_lk2)):
        lamvec = lamvec.at[r, :DA_QK_DIM].set(vec[l].astype(F32))
    return w_main, w_lr.astype(BF16), wg.astype(BF16), bg, lamvec


ROW_TILE = 512
ATTN_TILE = 512
GLA_ROWS = 512
FF_TILE = 2048


def _attn_slots(seq, tile):
    return min(NSLOT, max(2, seq // tile))


def _attn_tile(seq):
    tile = min(seq, ATTN_TILE)
    while tile > LANES:
        resident = 2 * 2 * seq * (LANES + DA_V_DIM + ONES_ROWS)
        tiles = tile * tile * (2 * _attn_slots(seq, tile) * (4 + 2) + 2 * 4)
        if resident + tiles <= VMEM_LIMIT * 3 // 4:
            break
        tile //= 2
    return tile


def _layer(x2d, p2d, batch, seq, lam_init, wts):
    tm = min(seq, ROW_TILE)
    proj, q_t, v_t, lr, kn = _in_proj(x2d, wts["pre_mix_g"], wts["w_main"], wts["w_lr"], batch, seq, tm)
    da = _diff_attention(proj, q_t, v_t, kn, wts["lamvec"], wts["da_norm_g"], batch, seq,
                         _attn_tile(seq), lam_init)
    o_f, o_b = _gla(proj, lr, wts["wg"], wts["bg"], batch, seq, min(seq, GLA_ROWS))
    x1, h2 = _out_proj(da, o_f, o_b, proj, x2d, wts["gla_norm_g"], wts["w_out"],
                       wts["post_mix_g"], wts["pre_ffn_g"], tm)
    return _ffn(h2, x1, p2d, wts["w_ffn_up"], wts["ffn_conv_w"], wts["ffn_conv_b"],
                wts["w_ffn_down"], wts["post_ffn_g"], wts["w_ple_gate"], wts["b_ple_gate"],
                wts["w_ple_proj"], wts["ple_norm_g"], seq, tm, FF_TILE)


def kernel(x_prompt, x_sample, p_prompt, p_sample, pre_mix_g, w_in, da_lq1, da_lk1, da_lq2, da_lk2, da_norm_g, gla_w_gate_f, gla_b_gate_f, gla_w_gate_b, gla_b_gate_b, gla_norm_g, w_out, post_mix_g, pre_ffn_g, w_ffn_up, ffn_conv_w, ffn_conv_b, w_ffn_down, post_ffn_g, w_ple_gate, b_ple_gate, w_ple_proj, ple_norm_g):
    depth = w_in.shape[0]
    layers = []
    for l in range(depth):
        w_main, w_lr, wg, bg, lamvec = _prep_weights(
            l, w_in, gla_w_gate_f, gla_b_gate_f, gla_w_gate_b, gla_b_gate_b,
            da_lq1, da_lk1, da_lq2, da_lk2)
        row = lambda a: a[l][None, :].astype(F32)
        layers.append(dict(
            w_main=w_main, w_lr=w_lr, wg=wg, bg=bg, lamvec=lamvec,
            pre_mix_g=row(pre_mix_g), da_norm_g=row(da_norm_g), gla_norm_g=row(gla_norm_g),
            w_out=w_out[l].astype(BF16), post_mix_g=row(post_mix_g), pre_ffn_g=row(pre_ffn_g),
            w_ffn_up=w_ffn_up[l].astype(BF16), ffn_conv_w=ffn_conv_w[l].astype(F32),
            ffn_conv_b=row(ffn_conv_b), w_ffn_down=w_ffn_down[l].astype(BF16),
            post_ffn_g=row(post_ffn_g), w_ple_gate=w_ple_gate[l].astype(BF16),
            b_ple_gate=row(b_ple_gate), w_ple_proj=w_ple_proj[l].astype(BF16),
            ple_norm_g=row(ple_norm_g)))

    def trunk(x, p):
        batch, seq, _ = x.shape
        t = batch * seq
        x2d = x.reshape(t, D_MODEL)
        for l in range(depth):
            lam_init = 0.8 - 0.6 * math.exp(-0.3 * l)
            x2d = _layer(x2d, p[l].reshape(t, PLE_DIM), batch, seq, lam_init, layers[l])
        return x2d.reshape(batch, seq, D_MODEL)

    return (trunk(x_prompt, p_prompt), trunk(x_sample, p_sample))
```

```python
import functools
import math

import jax
import jax.numpy as jnp
from jax import lax
from jax.experimental import pallas as pl
from jax.experimental.pallas import tpu as pltpu

F32 = jnp.float32
BF16 = jnp.bfloat16

D_MODEL = 1024
PLE_DIM = 256
DA_HEADS = 4
DA_QK_DIM = 64
DA_V_DIM = 128
DA_WIDTH = DA_HEADS * DA_V_DIM
GLA_HEADS = 4
GLA_DK = 64
GLA_DV = 128
GLA_QK_WIDTH = GLA_HEADS * GLA_DK
GLA_WIDTH = GLA_HEADS * GLA_DV
GLA_GATE_RANK = 16
GLA_GATE_NORM = 16.0
GLA_CHUNK = 64
D_FF = 4 * D_MODEL
NORM_EPS = 1e-6
LANES = 128
MAIN_WIDTH = 3 * DA_WIDTH + 2 * GLA_QK_WIDTH + 2 * GLA_WIDTH
NEG_BIG = -1e30
LOG2E = math.log2(math.e)
ZERO_PROB_GAP = 150.0
ONES_ROWS = 16
NSLOT = 2
VMEM_LIMIT = 56 * 1024 * 1024

COL_DA_Q = 0
COL_DA_K = DA_WIDTH
COL_DA_V = 2 * DA_WIDTH
COL_GLA = 3 * DA_WIDTH
PROJ_WIDTH = DA_WIDTH + 2 * GLA_QK_WIDTH + 2 * GLA_WIDTH
PCOL_DA_K = 0
PCOL_GQ = DA_WIDTH
PCOL_GK = PCOL_GQ + GLA_QK_WIDTH
PCOL_GV = PCOL_GK + GLA_QK_WIDTH
PCOL_GR = PCOL_GV + GLA_WIDTH


def _rms(x, g):
    return x * lax.rsqrt(jnp.mean(x * x, axis=-1, keepdims=True) + NORM_EPS) * g


def _nt_dot(a, b):
    return lax.dot_general(a, b, (((1,), (1,)), ((), ())), preferred_element_type=F32)


def _tn_dot(a, b):
    return lax.dot_general(a, b, (((0,), (0,)), ((), ())), preferred_element_type=F32)


def _dot(a, b):
    return jnp.dot(a, b, preferred_element_type=F32)


def _in_proj_body(x_ref, g_ref, wm_ref, wlr_ref, gsel_ref, proj_ref, qt_ref, vt_ref, lr_ref, kn_ref):
    h = _rms(x_ref[...], g_ref[...]).astype(BF16)
    main = _dot(h, wm_ref[...])
    proj_ref[:, :DA_WIDTH] = main[:, COL_DA_K:COL_DA_K + DA_WIDTH].astype(BF16)
    proj_ref[:, DA_WIDTH:] = main[:, COL_GLA:].astype(BF16)
    qt_ref[...] = main[:, COL_DA_Q:COL_DA_Q + DA_WIDTH].T.astype(BF16)
    for hd in range(DA_HEADS):
        c0 = COL_DA_V + hd * DA_V_DIM
        vt_ref[hd, :DA_V_DIM, :] = main[:, c0:c0 + DA_V_DIM].T.astype(BF16)
        vt_ref[hd, DA_V_DIM:, :] = jnp.ones((ONES_ROWS, main.shape[0]), BF16)
    lr_ref[...] = _dot(h, wlr_ref[...]).astype(BF16)
    k = main[:, COL_DA_K:COL_DA_K + DA_WIDTH]
    kn_ref[...] = jnp.max(_dot((k * k).astype(BF16), gsel_ref[...]), axis=0, keepdims=True)


def _in_proj(x2d, g, w_main, w_lr, batch, seq, tm):
    t = x2d.shape[0]
    per_seq = seq // tm
    va_rows = DA_V_DIM + ONES_ROWS
    col_group = jnp.arange(DA_WIDTH, dtype=jnp.int32)[:, None] // DA_QK_DIM
    gsel = (col_group == jnp.arange(LANES, dtype=jnp.int32)[None, :]).astype(BF16)
    return pl.pallas_call(
        _in_proj_body,
        grid=(t // tm,),
        in_specs=[
            pl.BlockSpec((tm, D_MODEL), lambda i: (i, 0)),
            pl.BlockSpec((1, D_MODEL), lambda i: (0, 0)),
            pl.BlockSpec((D_MODEL, MAIN_WIDTH), lambda i: (0, 0)),
            pl.BlockSpec((D_MODEL, LANES), lambda i: (0, 0)),
            pl.BlockSpec((DA_WIDTH, LANES), lambda i: (0, 0)),
        ],
        out_specs=[
            pl.BlockSpec((tm, PROJ_WIDTH), lambda i: (i, 0)),
            pl.BlockSpec((None, DA_WIDTH, tm), lambda i: (i // per_seq, 0, i % per_seq)),
            pl.BlockSpec((None, DA_HEADS, va_rows, tm),
                         lambda i: (i // per_seq, 0, 0, i % per_seq)),
            pl.BlockSpec((tm, LANES), lambda i: (i, 0)),
            pl.BlockSpec((None, 1, LANES), lambda i: (i, 0, 0)),
        ],
        out_shape=[
            jax.ShapeDtypeStruct((t, PROJ_WIDTH), BF16),
            jax.ShapeDtypeStruct((batch, DA_WIDTH, seq), BF16),
            jax.ShapeDtypeStruct((batch, DA_HEADS, va_rows, seq), BF16),
            jax.ShapeDtypeStruct((t, LANES), BF16),
            jax.ShapeDtypeStruct((t // tm, 1, LANES), F32),
        ],
        compiler_params=pltpu.CompilerParams(
            dimension_semantics=("arbitrary",), vmem_limit_bytes=VMEM_LIMIT),
        name="in_proj",
    )(x2d, g, w_main, w_lr, gsel)


def _split3(x):
    hi = x.astype(BF16).astype(F32)
    mid = (x - hi).astype(BF16).astype(F32)
    lo = (x - hi - mid).astype(BF16).astype(F32)
    return hi, mid, lo


def _attn_consts(tq):
    i = jnp.arange(tq, dtype=jnp.int32)
    lo = (i & 255).astype(F32)
    hi = (i - (i & 255)).astype(F32)
    slopes = jnp.asarray([2.0 ** (-8.0 * (h + 1) / DA_HEADS) for h in range(DA_HEADS)], F32) * LOG2E
    pieces = jnp.stack([p for piece in _split3(slopes) for p in (piece, piece)], axis=1)
    half = jnp.zeros((DA_HEADS, DA_QK_DIM, tq), F32).at[:, :6].set(
        jnp.broadcast_to(pieces[:, :, None], (DA_HEADS, 6, tq)))
    zeros = jnp.zeros_like(half)
    left = jnp.stack([jnp.concatenate([zeros, half], 1),
                      jnp.concatenate([half, zeros], 1)], axis=1)
    augq = jnp.stack([left, -left], axis=1).astype(BF16)
    kcols = jnp.stack([lo, hi, lo, hi, lo, hi], axis=-1)
    khalf = jnp.zeros((tq, DA_QK_DIM), F32).at[:, :6].set(kcols)
    kz = jnp.zeros_like(khalf)
    augk = jnp.stack([jnp.concatenate([kz, khalf], -1),
                      jnp.concatenate([khalf, kz], -1)], axis=0).astype(BF16)
    qb = slopes[:, None] * i.astype(F32)[None, :]
    qbias = jnp.stack([-qb, qb], axis=1)[:, :, None, :]
    d = (i[None, :] - i[:, None]).astype(F32)
    corr = 2.0 * slopes[:, None, None] * jnp.minimum(d, 0.0)[None]
    return augq, augk, qbias, corr


def _attn_body(lam_ref, qt_ref, k_ref, vt_ref, kn_ref, augq_ref, augk_ref, qbias_ref, corr_ref, gn_ref,
               o_ref, qa_ref, *scratch, tq, nk, nslot, lam_init):
    grab = lambda g: [scratch[2 * (g * nslot + sl):2 * (g * nslot + sl) + 2] for sl in range(nslot)]
    s_ref, p_ref, cmax_ref, alpha_ref = grab(0), grab(1), grab(2), grab(3)
    acc_ref, m_ref = scratch[8 * nslot:8 * nslot + 2], scratch[8 * nslot + 2:8 * nslot + 4]
    h = pl.program_id(1)
    qi = pl.program_id(2)
    slope = jnp.where(h == 0, 2.0 ** -2, jnp.where(h == 1, 2.0 ** -4,
                      jnp.where(h == 2, 2.0 ** -6, 2.0 ** -8))).astype(F32) * LOG2E

    qt = qt_ref[...]
    row = lax.broadcasted_iota(jnp.int32, (LANES, tq), 0)
    for side in range(2):
        qa_ref[2 * side] = jnp.where(row < DA_QK_DIM, qt, augq_ref[side, 0])
        qa_ref[2 * side + 1] = jnp.where(row >= DA_QK_DIM, qt, augq_ref[side, 1])
    for mp in range(2):
        m_ref[mp][...] = jnp.full(m_ref[mp].shape, NEG_BIG, F32)
        acc_ref[mp][...] = jnp.zeros(acc_ref[mp].shape, F32)
    lane = lax.broadcasted_iota(jnp.int32, (tq, LANES), 1)

    def scores(kt, side, slot, diag):
        k = k_ref[pl.ds(pl.multiple_of(kt * tq, tq), tq), :]
        ka = (jnp.where(lane < DA_QK_DIM, k, augk_ref[0]),
              jnp.where(lane >= DA_QK_DIM, k, augk_ref[1]))
        for mp in range(2):
            s = _dot(ka[mp], qa_ref[2 * side + mp])
            if diag:
                s = s + corr_ref[...]
            s_ref[slot][mp][...] = s
            cmax_ref[slot][mp][...] = jnp.max(s, axis=0, keepdims=True) + qbias_ref[side]

    scores(qi, 0, 0, True)

    qf = qt.astype(F32)
    lane1 = lax.broadcasted_iota(jnp.int32, (1, LANES), 1)
    kn2 = jnp.max(kn_ref[...], axis=0)
    gap = None
    for mp in range(2):
        rows = slice(mp * DA_QK_DIM, (mp + 1) * DA_QK_DIM)
        qn2 = jnp.max(jnp.sum(qf[rows] * qf[rows], axis=0, keepdims=True), axis=1, keepdims=True)
        kn2_mp = jnp.max(jnp.where(lane1 == 2 * h + mp, kn2, 0.0), axis=1, keepdims=True)
        m_min = jnp.min(cmax_ref[0][mp][...], axis=1, keepdims=True)
        g = 1.02 * jnp.sqrt(qn2 * kn2_mp) - m_min
        gap = g if gap is None else jnp.maximum(gap, g)
    far = ((gap + ZERO_PROB_GAP) / slope - 1.0) / tq
    reach = jnp.where(far < nk, jnp.maximum(jnp.ceil(far), 0.0), float(nk)).astype(jnp.int32)[0, 0]
    n_left = jnp.minimum(qi, reach)
    n_tiles = n_left + jnp.minimum(nk - 1 - qi, reach)

    def key_tile(t):
        return jnp.where(t <= n_left, qi - t, qi + (t - n_left))

    def tile_side(t):
        return (t > n_left).astype(jnp.int32)

    def tile_const(t):
        return -slope * (jnp.abs(qi - key_tile(t)) * tq).astype(F32)

    def probs(t, slot):
        shift = tile_const(t) + qbias_ref[tile_side(t)]
        for mp in range(2):
            m_old = m_ref[mp][...]
            m_new = jnp.maximum(m_old, cmax_ref[slot][mp][...] + tile_const(t))
            alpha_ref[slot][mp][...] = jnp.exp2(m_old - m_new)
            p_ref[slot][mp][...] = jnp.exp2(s_ref[slot][mp][...] - (m_new - shift)).astype(BF16)
            m_ref[mp][...] = m_new

    def accumulate(t, slot):
        va = vt_ref[:, pl.ds(pl.multiple_of(key_tile(t) * tq, tq), tq)]
        for mp in range(2):
            acc_ref[mp][...] = (alpha_ref[slot][mp][...] * acc_ref[mp][...]
                                + _dot(va, p_ref[slot][mp][...]))

    def step(t, slot):
        prev = (slot - 1) % nslot

        @pl.when(t <= n_tiles)
        def _():
            probs(t - 1, prev)
            scores(key_tile(t), tile_side(t), slot, False)
            accumulate(t - 1, prev)

    def trip(i, carry):
        for r in range(1, nslot + 1):
            step(nslot * i + r, r % nslot)
        return carry

    lax.fori_loop(0, (n_tiles + nslot - 1) // nslot, trip, 0)
    for r in range(nslot):
        @pl.when(n_tiles % nslot == r)
        def _(r=r):
            probs(n_tiles, r)
            accumulate(n_tiles, r)

    lv = lam_ref[...]
    lam = (jnp.exp(jnp.sum(lv[0:1] * lv[1:2], axis=-1, keepdims=True))
           - jnp.exp(jnp.sum(lv[2:3] * lv[3:4], axis=-1, keepdims=True)) + lam_init)
    a0 = acc_ref[0][...]
    a1 = acc_ref[1][...]
    o0 = a0[:DA_V_DIM] * (1.0 / a0[DA_V_DIM:DA_V_DIM + 1])
    o1 = a1[:DA_V_DIM] * (1.0 / a1[DA_V_DIM:DA_V_DIM + 1])
    out = o0 - lam * o1
    inv = lax.rsqrt(jnp.mean(out * out, axis=0, keepdims=True) + NORM_EPS)
    out = out * inv * gn_ref[...] * (1.0 - lam_init)
    o_ref[...] = out.T.astype(o_ref.dtype)


def _diff_attention(proj, q_t, v_t, kn, lamvec, gn, batch, seq, tq, lam_init):
    t = proj.shape[0]
    nk = seq // tq
    augq, augk, qbias, corr = _attn_consts(tq)
    gnb = jnp.broadcast_to(gn.reshape(DA_V_DIM, 1), (DA_V_DIM, tq))
    nslot = _attn_slots(seq, tq)
    body = functools.partial(_attn_body, tq=tq, nk=nk, nslot=nslot, lam_init=lam_init)
    kblk = PCOL_DA_K // LANES
    return pl.pallas_call(
        body,
        grid=(batch, DA_HEADS, nk),
        in_specs=[
            pl.BlockSpec((8, LANES), lambda b, h, i: (0, 0)),
            pl.BlockSpec((None, LANES, tq), lambda b, h, i: (b, h, i)),
            pl.BlockSpec((seq, LANES), lambda b, h, i: (b, kblk + h)),
            pl.BlockSpec((None, None, DA_V_DIM + ONES_ROWS, seq), lambda b, h, i: (b, h, 0, 0)),
            pl.BlockSpec((kn.shape[0] // batch, 1, LANES), lambda b, h, i: (b, 0, 0)),
            pl.BlockSpec((None, 2, 2, LANES, tq), lambda b, h, i: (h, 0, 0, 0, 0)),
            pl.BlockSpec((2, tq, LANES), lambda b, h, i: (0, 0, 0)),
            pl.BlockSpec((None, 2, 1, tq), lambda b, h, i: (h, 0, 0, 0)),
            pl.BlockSpec((None, tq, tq), lambda b, h, i: (h, 0, 0)),
            pl.BlockSpec((DA_V_DIM, tq), lambda b, h, i: (0, 0)),
        ],
        out_specs=pl.BlockSpec((tq, LANES), lambda b, h, i: (b * nk + i, h)),
        out_shape=jax.ShapeDtypeStruct((t, DA_WIDTH), BF16),
        scratch_shapes=[
            pltpu.VMEM((4, LANES, tq), BF16),
        ] + [pltpu.VMEM((tq, tq), F32)] * (2 * nslot)
        + [pltpu.VMEM((tq, tq), BF16)] * (2 * nslot)
        + [pltpu.VMEM((1, tq), F32)] * (2 * nslot)
        + [pltpu.VMEM((1, tq), F32)] * (2 * nslot)
        + [pltpu.VMEM((DA_V_DIM + ONES_ROWS, tq), F32)] * 2
        + [pltpu.VMEM((1, tq), F32)] * 2,
        compiler_params=pltpu.CompilerParams(
            dimension_semantics=("arbitrary", "arbitrary", "arbitrary"),
            vmem_limit_bytes=VMEM_LIMIT),
        name="diff_attention",
    )(lamvec, q_t, proj, v_t, kn, augq, augk, qbias, corr, gnb)


def _log_sigmoid(z):
    return jnp.minimum(z, 0.0) - jnp.log1p(jnp.exp(-jnp.abs(z)))


def _gla_chunk(q, k, v, g, st_ref, reverse):
    c = GLA_CHUNK
    q = q.astype(F32)
    k = k.astype(F32)
    row = lax.broadcasted_iota(jnp.int32, (c, GLA_QK_WIDTH), 0)
    lane = lax.broadcasted_iota(jnp.int32, (c, GLA_QK_WIDTH), 1)
    b = g
    shift = 1
    while shift < c:
        b = b + jnp.where(row >= shift, pltpu.roll(b, shift, 0), 0.0)
        shift *= 2
    tot = b[c - 1:c, :]
    if reverse:
        b = tot - b + g
    q_t = (q * jnp.exp(b)).astype(BF16)
    k_t = (k * jnp.exp(-b)).astype(BF16)
    k_end = (k * jnp.exp(tot - b)).astype(BF16)
    decay = jnp.exp(tot)

    head = lane // GLA_DK
    zero_k = jnp.zeros_like(k_t)
    k_bd = jnp.concatenate([jnp.where(head == h, k_t, zero_k) for h in range(GLA_HEADS)], axis=0)
    att = _nt_dot(q_t, k_bd)
    pos = lane % c
    keep = (pos > row) if reverse else (pos <= row)
    att = jnp.where(keep, att, 0.0).astype(BF16)

    vhead = lax.broadcasted_iota(jnp.int32, (c, GLA_WIDTH), 1) // GLA_DV
    zero_v = jnp.zeros_like(v)
    v_bd = jnp.concatenate([jnp.where(vhead == h, v, zero_v) for h in range(GLA_HEADS)], axis=0)
    st = st_ref[...]
    st_b = st.astype(BF16)
    shead = lax.broadcasted_iota(jnp.int32, st.shape, 1) // GLA_DK
    zero_s = jnp.zeros_like(st_b)
    st_bd = jnp.concatenate([jnp.where(shead == h, st_b, zero_s) for h in range(GLA_HEADS)], axis=0)
    out = _dot(att, v_bd) + _nt_dot(q_t, st_bd)

    kv_t = _tn_dot(v, k_end)
    upd = kv_t[(GLA_HEADS - 1) * GLA_DV:]
    for h in range(GLA_HEADS - 2, -1, -1):
        upd = jnp.where(shead == h, kv_t[h * GLA_DV:(h + 1) * GLA_DV], upd)
    st_ref[...] = decay * st + upd
    return out


def _gla_body(qf_ref, kf_ref, vf_ref, lrf_ref, qb_ref, kb_ref, vb_ref, lrb_ref, wg_ref, bg_ref,
              of_ref, ob_ref, stf_ref, stb_ref, *, rows):
    @pl.when(pl.program_id(1) == 0)
    def _():
        stf_ref[...] = jnp.zeros(stf_ref.shape, F32)
        stb_ref[...] = jnp.zeros(stb_ref.shape, F32)

    wg = wg_ref[...]
    bg = bg_ref[...]
    zf = _dot(lrf_ref[...], wg[:, :GLA_QK_WIDTH]) + bg[:, :GLA_QK_WIDTH]
    zb = _dot(lrb_ref[...], wg[:, GLA_QK_WIDTH:]) + bg[:, GLA_QK_WIDTH:]
    gf = _log_sigmoid(zf) / GLA_GATE_NORM
    gb = _log_sigmoid(zb) / GLA_GATE_NORM
    nc = rows // GLA_CHUNK
    for ci in range(nc):
        sf = slice(ci * GLA_CHUNK, (ci + 1) * GLA_CHUNK)
        of_ref[sf, :] = _gla_chunk(qf_ref[sf, :], kf_ref[sf, :], vf_ref[sf, :], gf[sf, :],
                                   stf_ref, False).astype(of_ref.dtype)
        cj = nc - 1 - ci
        sb = slice(cj * GLA_CHUNK, (cj + 1) * GLA_CHUNK)
        ob_ref[sb, :] = _gla_chunk(qb_ref[sb, :], kb_ref[sb, :], vb_ref[sb, :], gb[sb, :],
                                   stb_ref, True).astype(ob_ref.dtype)


def _gla(proj, lr, wg, bg, batch, seq, rows):
    t = proj.shape[0]
    nb = seq // rows
    qblk = PCOL_GQ // GLA_QK_WIDTH
    kblk = PCOL_GK // GLA_QK_WIDTH
    vblk = PCOL_GV // GLA_WIDTH

    def fwd(col):
        return lambda b, i: (b * nb + i, col)

    def bwd(col):
        return lambda b, i: (b * nb + nb - 1 - i, col)

    def specs(ix):
        return [
            pl.BlockSpec((rows, GLA_QK_WIDTH), ix(qblk)),
            pl.BlockSpec((rows, GLA_QK_WIDTH), ix(kblk)),
            pl.BlockSpec((rows, GLA_WIDTH), ix(vblk)),
            pl.BlockSpec((rows, LANES), ix(0)),
        ]

    return pl.pallas_call(
        functools.partial(_gla_body, rows=rows),
        grid=(batch, nb),
        in_specs=specs(fwd) + specs(bwd) + [
            pl.BlockSpec((LANES, 2 * GLA_QK_WIDTH), lambda b, i: (0, 0)),
            pl.BlockSpec((1, 2 * GLA_QK_WIDTH), lambda b, i: (0, 0)),
        ],
        out_specs=[
            pl.BlockSpec((rows, GLA_WIDTH), fwd(0)),
            pl.BlockSpec((rows, GLA_WIDTH), bwd(0)),
        ],
        out_shape=[jax.ShapeDtypeStruct((t, GLA_WIDTH), BF16)] * 2,
        scratch_shapes=[pltpu.VMEM((GLA_DV, GLA_QK_WIDTH), F32)] * 2,
        compiler_params=pltpu.CompilerParams(
            dimension_semantics=("arbitrary", "arbitrary"), vmem_limit_bytes=VMEM_LIMIT),
        name="gla",
    )(proj, proj, proj, lr, proj, proj, proj, lr, wg, bg)


def _out_proj_body(da_ref, of_ref, ob_ref, gr_ref, x_ref, gng_ref, wout_ref, pmg_ref, pfg_ref,
                   x1_ref, h2_ref):
    o = of_ref[...].astype(F32) + ob_ref[...].astype(F32)
    gr = gr_ref[...].astype(F32)
    parts = []
    for h in range(GLA_HEADS):
        sl = slice(h * GLA_DV, (h + 1) * GLA_DV)
        gh = gr[:, sl]
        parts.append(_rms(o[:, sl], gng_ref[...]) * (gh * jax.nn.sigmoid(gh)))
    gla = jnp.concatenate(parts, axis=1).astype(BF16)
    mix = _dot(jnp.concatenate([da_ref[...], gla], axis=1), wout_ref[...])
    x1 = x_ref[...] + _rms(mix, pmg_ref[...])
    x1_ref[...] = x1
    h2_ref[...] = _rms(x1, pfg_ref[...]).astype(BF16)


def _out_proj(da, o_f, o_b, proj, x2d, gng, w_out, pmg, pfg, tm):
    t = x2d.shape[0]
    row = lambda i: (i, 0)
    const = lambda i: (0, 0)
    return pl.pallas_call(
        _out_proj_body,
        grid=(t // tm,),
        in_specs=[
            pl.BlockSpec((tm, DA_WIDTH), row),
            pl.BlockSpec((tm, GLA_WIDTH), row),
            pl.BlockSpec((tm, GLA_WIDTH), row),
            pl.BlockSpec((tm, GLA_WIDTH), lambda i: (i, PCOL_GR // GLA_WIDTH)),
            pl.BlockSpec((tm, D_MODEL), row),
            pl.BlockSpec((1, GLA_DV), const),
            pl.BlockSpec((D_MODEL, D_MODEL), const),
            pl.BlockSpec((1, D_MODEL), const),
            pl.BlockSpec((1, D_MODEL), const),
        ],
        out_specs=[pl.BlockSpec((tm, D_MODEL), row), pl.BlockSpec((tm, D_MODEL), row)],
        out_shape=[jax.ShapeDtypeStruct((t, D_MODEL), F32),
                   jax.ShapeDtypeStruct((t, D_MODEL), BF16)],
        compiler_params=pltpu.CompilerParams(
            dimension_semantics=("arbitrary",), vmem_limit_bytes=VMEM_LIMIT),
        name="out_proj",
    )(da, o_f, o_b, proj, x2d, gng, w_out, pmg, pfg)


HALO = 16


def _ffn_body(h_ref, hp_ref, hn_ref, wg_ref, wu_ref, cw_ref, cb_ref, wd_ref, x1_ref, p_ref,
              pfg_ref, wpg_ref, bpg_ref, wpp_ref, png_ref, o_ref, hext_ref, act_ref,
              *, tm, tf, seq, nff):
    i = pl.program_id(0)
    j = pl.program_id(1)

    @pl.when(j == 0)
    def _():
        first = (i * tm) % seq == 0
        last = ((i + 1) * tm) % seq == 0
        hp = hp_ref[...]
        hn = hn_ref[...]
        hext_ref[0:HALO, :] = jnp.where(first, jnp.zeros_like(hp), hp)
        hext_ref[HALO:HALO + tm, :] = h_ref[...]
        hext_ref[HALO + tm:, :] = jnp.where(last, jnp.zeros_like(hn), hn)

    gate = _dot(hext_ref[...], wg_ref[...])
    ext = tm + 2 * HALO
    cw = cw_ref[...]
    conv = (pltpu.roll(gate, 1, 0)[HALO:HALO + tm] * cw[0:1]
            + gate[HALO:HALO + tm] * cw[1:2]
            + pltpu.roll(gate, ext - 1, 0)[HALO:HALO + tm] * cw[2:3]
            + cb_ref[...])
    up = _dot(h_ref[...], wu_ref[...])
    act_ref[:, pl.ds(pl.multiple_of(j * tf, tf), tf)] = (jax.nn.gelu(conv) * up).astype(BF16)

    @pl.when(j == nff - 1)
    def _():
        x2 = x1_ref[...] + _rms(_dot(act_ref[...], wd_ref[...]), pfg_ref[...])
        e = _rms(_dot(p_ref[...].astype(BF16), wpp_ref[...]), png_ref[...])
        gate_e = jax.nn.sigmoid(_dot(x2.astype(BF16), wpg_ref[...]) + bpg_ref[...])
        o_ref[...] = x2 + gate_e * e


def _ffn(h2, x1, p2d, w_up, conv_w, conv_b, w_down, pfg, w_pg, b_pg, w_pp, png, seq, tm, tf):
    t = h2.shape[0]
    nff = D_FF // tf
    nhalo = t // HALO
    per = tm // HALO
    row = lambda i, j: (i, 0)
    const = lambda i, j: (0, 0)
    return pl.pallas_call(
        functools.partial(_ffn_body, tm=tm, tf=tf, seq=seq, nff=nff),
        grid=(t // tm, nff),
        in_specs=[
            pl.BlockSpec((tm, D_MODEL), row),
            pl.BlockSpec((HALO, D_MODEL), lambda i, j: (jnp.maximum(i * per - 1, 0), 0)),
            pl.BlockSpec((HALO, D_MODEL), lambda i, j: (jnp.minimum((i + 1) * per, nhalo - 1), 0)),
            pl.BlockSpec((D_MODEL, tf), lambda i, j: (0, j)),
            pl.BlockSpec((D_MODEL, tf), lambda i, j: (0, nff + j)),
            pl.BlockSpec((3, tf), lambda i, j: (0, j)),
            pl.BlockSpec((1, tf), lambda i, j: (0, j)),
            pl.BlockSpec((D_FF, D_MODEL), const, pipeline_mode=pl.Buffered(1)),
            pl.BlockSpec((tm, D_MODEL), row),
            pl.BlockSpec((tm, PLE_DIM), row),
            pl.BlockSpec((1, D_MODEL), const),
            pl.BlockSpec((D_MODEL, D_MODEL), const, pipeline_mode=pl.Buffered(1)),
            pl.BlockSpec((1, D_MODEL), const),
            pl.BlockSpec((PLE_DIM, D_MODEL), const, pipeline_mode=pl.Buffered(1)),
            pl.BlockSpec((1, D_MODEL), const),
        ],
        out_specs=pl.BlockSpec((tm, D_MODEL), row),
        out_shape=jax.ShapeDtypeStruct((t, D_MODEL), F32),
        scratch_shapes=[
            pltpu.VMEM((tm + 2 * HALO, D_MODEL), BF16),
            pltpu.VMEM((tm, D_FF), BF16),
        ],
        compiler_params=pltpu.CompilerParams(
            dimension_semantics=("arbitrary", "arbitrary"), vmem_limit_bytes=VMEM_LIMIT),
        name="conv_ffn",
    )(h2, h2, h2, w_up, w_up, conv_w, conv_b, w_down, x1, p2d, pfg, w_pg, b_pg, w_pp, png)


def _prep_weights(l, w_in, gla_w_gate_f, gla_b_gate_f, gla_w_gate_b, gla_b_gate_b, da_lq1,
                  da_lk1, da_lq2, da_lk2):
    w = w_in[l]
    qscale = jnp.concatenate([
        jnp.full((DA_WIDTH,), DA_QK_DIM ** -0.5 * LOG2E, F32), jnp.ones((2 * DA_WIDTH,), F32),
        jnp.full((GLA_QK_WIDTH,), GLA_DK ** -0.5, F32),
        jnp.ones((GLA_QK_WIDTH + 2 * GLA_WIDTH,), F32)])
    w_main = (w[:, :MAIN_WIDTH] * qscale).astype(BF16)
    w_lr = jnp.zeros((D_MODEL, LANES), F32).at[:, :2 * GLA_GATE_RANK].set(w[:, MAIN_WIDTH:])
    wg = jnp.zeros((LANES, 2 * GLA_QK_WIDTH), F32)
    wg = wg.at[:GLA_GATE_RANK, :GLA_QK_WIDTH].set(gla_w_gate_f[l])
    wg = wg.at[GLA_GATE_RANK:2 * GLA_GATE_RANK, GLA_QK_WIDTH:].set(gla_w_gate_b[l])
    bg = jnp.concatenate([gla_b_gate_f[l], gla_b_gate_b[l]])[None, :]
    lamvec = jnp.zeros((8, LANES), F32)
    for r, vec in enumerate((da_lq1, da_lk1, da_lq2, da_lk2)):
        lamvec = lamvec.at[r, :DA_QK_DIM].set(vec[l].astype(F32))
    return w_main, w_lr.astype(BF16), wg.astype(BF16), bg, lamvec


ROW_TILE = 512
ATTN_TILE = 1024
GLA_ROWS = 512
FF_TILE = 2048


def _attn_slots(seq, tile):
    return min(NSLOT, max(2, seq // tile))


def _attn_tile(seq):
    tile = min(seq, ATTN_TILE)
    while tile > LANES:
        resident = 2 * 2 * seq * (LANES + DA_V_DIM + ONES_ROWS)
        tiles = tile * tile * (2 * _attn_slots(seq, tile) * (4 + 2) + 2 * 4)
        if resident + tiles <= VMEM_LIMIT * 3 // 4:
            break
        tile //= 2
    return tile


def _layer(x2d, p2d, batch, seq, lam_init, wts):
    tm = min(seq, ROW_TILE)
    proj, q_t, v_t, lr, kn = _in_proj(x2d, wts["pre_mix_g"], wts["w_main"], wts["w_lr"], batch, seq, tm)
    da = _diff_attention(proj, q_t, v_t, kn, wts["lamvec"], wts["da_norm_g"], batch, seq,
                         _attn_tile(seq), lam_init)
    o_f, o_b = _gla(proj, lr, wts["wg"], wts["bg"], batch, seq, min(seq, GLA_ROWS))
    x1, h2 = _out_proj(da, o_f, o_b, proj, x2d, wts["gla_norm_g"], wts["w_out"],
                       wts["post_mix_g"], wts["pre_ffn_g"], tm)
    return _ffn(h2, x1, p2d, wts["w_ffn_up"], wts["ffn_conv_w"], wts["ffn_conv_b"],
                wts["w_ffn_down"], wts["post_ffn_g"], wts["w_ple_gate"], wts["b_ple_gate"],
                wts["w_ple_proj"], wts["ple_norm_g"], seq, tm, FF_TILE)


def kernel(x_prompt, x_sample, p_prompt, p_sample, pre_mix_g, w_in, da_lq1, da_lk1, da_lq2, da_lk2, da_norm_g, gla_w_gate_f, gla_b_gate_f, gla_w_gate_b, gla_b_gate_b, gla_norm_g, w_out, post_mix_g, pre_ffn_g, w_ffn_up, ffn_conv_w, ffn_conv_b, w_ffn_down, post_ffn_g, w_ple_gate, b_ple_gate, w_ple_proj, ple_norm_g):
    depth = w_in.shape[0]
    layers = []
    for l in range(depth):
        w_main, w_lr, wg, bg, lamvec = _prep_weights(
            l, w_in, gla_w_gate_f, gla_b_gate_f, gla_w_gate_b, gla_b_gate_b,
            da_lq1, da_lk1, da_lq2, da_lk2)
        row = lambda a: a[l][None, :].astype(F32)
        layers.append(dict(
            w_main=w_main, w_lr=w_lr, wg=wg, bg=bg, lamvec=lamvec,
            pre_mix_g=row(pre_mix_g), da_norm_g=row(da_norm_g), gla_norm_g=row(gla_norm_g),
            w_out=w_out[l].astype(BF16), post_mix_g=row(post_mix_g), pre_ffn_g=row(pre_ffn_g),
            w_ffn_up=w_ffn_up[l].astype(BF16), ffn_conv_w=ffn_conv_w[l].astype(F32),
            ffn_conv_b=row(ffn_conv_b), w_ffn_down=w_ffn_down[l].astype(BF16),
            post_ffn_g=row(post_ffn_g), w_ple_gate=w_ple_gate[l].astype(BF16),
            b_ple_gate=row(b_ple_gate), w_ple_proj=w_ple_proj[l].astype(BF16),
            ple_norm_g=row(ple_norm_g)))

    def trunk(x, p):
        batch, seq, _ = x.shape
        t = batch * seq
        x2d = x.reshape(t, D_MODEL)
        for l in range(depth):
            lam_init = 0.8 - 0.6 * math.exp(-0.3 * l)
            x2d = _layer(x2d, p[l].reshape(t, PLE_DIM), batch, seq, lam_init, layers[l])
        return x2d.reshape(batch, seq, D_MODEL)

    return (trunk(x_prompt, p_prompt), trunk(x_sample, p_sample))
```

```python
import functools
import math

import jax
import jax.numpy as jnp
from jax import lax
from jax.experimental import pallas as pl
from jax.experimental.pallas import tpu as pltpu

F32 = jnp.float32
BF16 = jnp.bfloat16

D_MODEL = 1024
PLE_DIM = 256
DA_HEADS = 4
DA_QK_DIM = 64
DA_V_DIM = 128
DA_WIDTH = DA_HEADS * DA_V_DIM
GLA_HEADS = 4
GLA_DK = 64
GLA_DV = 128
GLA_QK_WIDTH = GLA_HEADS * GLA_DK
GLA_WIDTH = GLA_HEADS * GLA_DV
GLA_GATE_RANK = 16
GLA_GATE_NORM = 16.0
GLA_CHUNK = 64
D_FF = 4 * D_MODEL
NORM_EPS = 1e-6
LANES = 128
MAIN_WIDTH = 3 * DA_WIDTH + 2 * GLA_QK_WIDTH + 2 * GLA_WIDTH
NEG_BIG = -1e30
LOG2E = math.log2(math.e)
ZERO_PROB_GAP = 138.0
ONES_ROWS = 16
NSLOT = 2
VMEM_LIMIT = 56 * 1024 * 1024

COL_DA_Q = 0
COL_DA_K = DA_WIDTH
COL_DA_V = 2 * DA_WIDTH
COL_GLA = 3 * DA_WIDTH
PROJ_WIDTH = DA_WIDTH + 2 * GLA_QK_WIDTH + 2 * GLA_WIDTH
PCOL_DA_K = 0
PCOL_GQ = DA_WIDTH
PCOL_GK = PCOL_GQ + GLA_QK_WIDTH
PCOL_GV = PCOL_GK + GLA_QK_WIDTH
PCOL_GR = PCOL_GV + GLA_WIDTH


def _rms(x, g):
    return x * lax.rsqrt(jnp.mean(x * x, axis=-1, keepdims=True) + NORM_EPS) * g


def _nt_dot(a, b):
    return lax.dot_general(a, b, (((1,), (1,)), ((), ())), preferred_element_type=F32)


def _tn_dot(a, b):
    return lax.dot_general(a, b, (((0,), (0,)), ((), ())), preferred_element_type=F32)


def _dot(a, b):
    return jnp.dot(a, b, preferred_element_type=F32)


def _in_proj_body(x_ref, g_ref, wm_ref, wlr_ref, gsel_ref, proj_ref, qt_ref, vt_ref, lr_ref, kn_ref):
    h = _rms(x_ref[...], g_ref[...]).astype(BF16)
    main = _dot(h, wm_ref[...])
    proj_ref[:, :DA_WIDTH] = main[:, COL_DA_K:COL_DA_K + DA_WIDTH].astype(BF16)
    proj_ref[:, DA_WIDTH:] = main[:, COL_GLA:].astype(BF16)
    qt_ref[...] = main[:, COL_DA_Q:COL_DA_Q + DA_WIDTH].T.astype(BF16)
    for hd in range(DA_HEADS):
        c0 = COL_DA_V + hd * DA_V_DIM
        vt_ref[hd, :DA_V_DIM, :] = main[:, c0:c0 + DA_V_DIM].T.astype(BF16)
        vt_ref[hd, DA_V_DIM:, :] = jnp.ones((ONES_ROWS, main.shape[0]), BF16)
    lr_ref[...] = _dot(h, wlr_ref[...]).astype(BF16)
    k = main[:, COL_DA_K:COL_DA_K + DA_WIDTH]
    kn_ref[...] = jnp.max(_dot((k * k).astype(BF16), gsel_ref[...]), axis=0, keepdims=True)


def _in_proj(x2d, g, w_main, w_lr, batch, seq, tm):
    t = x2d.shape[0]
    per_seq = seq // tm
    va_rows = DA_V_DIM + ONES_ROWS
    col_group = jnp.arange(DA_WIDTH, dtype=jnp.int32)[:, None] // DA_QK_DIM
    gsel = (col_group == jnp.arange(LANES, dtype=jnp.int32)[None, :]).astype(BF16)
    return pl.pallas_call(
        _in_proj_body,
        grid=(t // tm,),
        in_specs=[
            pl.BlockSpec((tm, D_MODEL), lambda i: (i, 0)),
            pl.BlockSpec((1, D_MODEL), lambda i: (0, 0)),
            pl.BlockSpec((D_MODEL, MAIN_WIDTH), lambda i: (0, 0)),
            pl.BlockSpec((D_MODEL, LANES), lambda i: (0, 0)),
            pl.BlockSpec((DA_WIDTH, LANES), lambda i: (0, 0)),
        ],
        out_specs=[
            pl.BlockSpec((tm, PROJ_WIDTH), lambda i: (i, 0)),
            pl.BlockSpec((None, DA_WIDTH, tm), lambda i: (i // per_seq, 0, i % per_seq)),
            pl.BlockSpec((None, DA_HEADS, va_rows, tm),
                         lambda i: (i // per_seq, 0, 0, i % per_seq)),
            pl.BlockSpec((tm, LANES), lambda i: (i, 0)),
            pl.BlockSpec((None, 1, LANES), lambda i: (i, 0, 0)),
        ],
        out_shape=[
            jax.ShapeDtypeStruct((t, PROJ_WIDTH), BF16),
            jax.ShapeDtypeStruct((batch, DA_WIDTH, seq), BF16),
            jax.ShapeDtypeStruct((batch, DA_HEADS, va_rows, seq), BF16),
            jax.ShapeDtypeStruct((t, LANES), BF16),
            jax.ShapeDtypeStruct((t // tm, 1, LANES), F32),
        ],
        compiler_params=pltpu.CompilerParams(
            dimension_semantics=("arbitrary",), vmem_limit_bytes=VMEM_LIMIT),
        name="in_proj",
    )(x2d, g, w_main, w_lr, gsel)


def _split3(x):
    hi = x.astype(BF16).astype(F32)
    mid = (x - hi).astype(BF16).astype(F32)
    lo = (x - hi - mid).astype(BF16).astype(F32)
    return hi, mid, lo


def _attn_consts(tq):
    i = jnp.arange(tq, dtype=jnp.int32)
    lo = (i & 255).astype(F32)
    hi = (i - (i & 255)).astype(F32)
    slopes = jnp.asarray([2.0 ** (-8.0 * (h + 1) / DA_HEADS) for h in range(DA_HEADS)], F32) * LOG2E
    pieces = jnp.stack([p for piece in _split3(slopes) for p in (piece, piece)], axis=1)
    half = jnp.zeros((DA_HEADS, DA_QK_DIM, tq), F32).at[:, :6].set(
        jnp.broadcast_to(pieces[:, :, None], (DA_HEADS, 6, tq)))
    zeros = jnp.zeros_like(half)
    left = jnp.stack([jnp.concatenate([zeros, half], 1),
                      jnp.concatenate([half, zeros], 1)], axis=1)
    augq = jnp.stack([left, -left], axis=1).astype(BF16)
    kcols = jnp.stack([lo, hi, lo, hi, lo, hi], axis=-1)
    khalf = jnp.zeros((tq, DA_QK_DIM), F32).at[:, :6].set(kcols)
    kz = jnp.zeros_like(khalf)
    augk = jnp.stack([jnp.concatenate([kz, khalf], -1),
                      jnp.concatenate([khalf, kz], -1)], axis=0).astype(BF16)
    qb = slopes[:, None] * i.astype(F32)[None, :]
    qbias = jnp.stack([-qb, qb], axis=1)[:, :, None, :]
    d = (i[None, :] - i[:, None]).astype(F32)
    corr = 2.0 * slopes[:, None, None] * jnp.minimum(d, 0.0)[None]
    return augq, augk, qbias, corr


def _attn_body(lam_ref, qt_ref, k_ref, vt_ref, kn_ref, augq_ref, augk_ref, qbias_ref, corr_ref, gn_ref,
               o_ref, qa_ref, *scratch, tq, nk, nslot, lam_init):
    grab = lambda g: [scratch[2 * (g * nslot + sl):2 * (g * nslot + sl) + 2] for sl in range(nslot)]
    s_ref, p_ref, cmax_ref, alpha_ref = grab(0), grab(1), grab(2), grab(3)
    acc_ref, m_ref = scratch[8 * nslot:8 * nslot + 2], scratch[8 * nslot + 2:8 * nslot + 4]
    h = pl.program_id(1)
    qi = pl.program_id(2)
    slope = jnp.where(h == 0, 2.0 ** -2, jnp.where(h == 1, 2.0 ** -4,
                      jnp.where(h == 2, 2.0 ** -6, 2.0 ** -8))).astype(F32) * LOG2E

    qt = qt_ref[...]
    row = lax.broadcasted_iota(jnp.int32, (LANES, tq), 0)
    for side in range(2):
        qa_ref[2 * side] = jnp.where(row < DA_QK_DIM, qt, augq_ref[side, 0])
        qa_ref[2 * side + 1] = jnp.where(row >= DA_QK_DIM, qt, augq_ref[side, 1])
    for mp in range(2):
        m_ref[mp][...] = jnp.full(m_ref[mp].shape, NEG_BIG, F32)
        acc_ref[mp][...] = jnp.zeros(acc_ref[mp].shape, F32)
    lane = lax.broadcasted_iota(jnp.int32, (tq, LANES), 1)

    def scores(kt, side, slot, diag):
        k = k_ref[pl.ds(pl.multiple_of(kt * tq, tq), tq), :]
        ka = (jnp.where(lane < DA_QK_DIM, k, augk_ref[0]),
              jnp.where(lane >= DA_QK_DIM, k, augk_ref[1]))
        for mp in range(2):
            s = _dot(ka[mp], qa_ref[2 * side + mp])
            if diag:
                s = s + corr_ref[...]
            s_ref[slot][mp][...] = s
            cmax_ref[slot][mp][...] = jnp.max(s, axis=0, keepdims=True) + qbias_ref[side]

    scores(qi, 0, 0, True)

    qf = qt.astype(F32)
    lane1 = lax.broadcasted_iota(jnp.int32, (1, LANES), 1)
    kn2 = jnp.max(kn_ref[...], axis=0)
    gap = None
    for mp in range(2):
        rows = slice(mp * DA_QK_DIM, (mp + 1) * DA_QK_DIM)
        qn2 = jnp.sum(qf[rows] * qf[rows], axis=0, keepdims=True)
        kn2_mp = jnp.max(jnp.where(lane1 == 2 * h + mp, kn2, 0.0), axis=1, keepdims=True)
        g = jnp.max(1.02 * jnp.sqrt(qn2 * kn2_mp) - cmax_ref[0][mp][...], axis=1, keepdims=True)
        gap = g if gap is None else jnp.maximum(gap, g)
    far = ((gap + ZERO_PROB_GAP) / slope - 1.0) / tq
    reach = jnp.where(far < nk, jnp.maximum(jnp.ceil(far), 0.0), float(nk)).astype(jnp.int32)[0, 0]
    n_left = jnp.minimum(qi, reach)
    n_tiles = n_left + jnp.minimum(nk - 1 - qi, reach)

    def key_tile(t):
        return jnp.where(t <= n_left, qi - t, qi + (t - n_left))

    def tile_side(t):
        return (t > n_left).astype(jnp.int32)

    def tile_const(t):
        return -slope * (jnp.abs(qi - key_tile(t)) * tq).astype(F32)

    def probs(t, slot):
        shift = tile_const(t) + qbias_ref[tile_side(t)]
        for mp in range(2):
            m_old = m_ref[mp][...]
            m_new = jnp.maximum(m_old, cmax_ref[slot][mp][...] + tile_const(t))
            alpha_ref[slot][mp][...] = jnp.exp2(m_old - m_new)
            p_ref[slot][mp][...] = jnp.exp2(s_ref[slot][mp][...] - (m_new - shift)).astype(BF16)
            m_ref[mp][...] = m_new

    def accumulate(t, slot):
        va = vt_ref[:, pl.ds(pl.multiple_of(key_tile(t) * tq, tq), tq)]
        for mp in range(2):
            acc_ref[mp][...] = (alpha_ref[slot][mp][...] * acc_ref[mp][...]
                                + _dot(va, p_ref[slot][mp][...]))

    def step(t, slot):
        prev = (slot - 1) % nslot

        @pl.when(t <= n_tiles)
        def _():
            probs(t - 1, prev)
            scores(key_tile(t), tile_side(t), slot, False)
            accumulate(t - 1, prev)

    def trip(i, carry):
        for r in range(1, nslot + 1):
            step(nslot * i + r, r % nslot)
        return carry

    lax.fori_loop(0, (n_tiles + nslot - 1) // nslot, trip, 0)
    for r in range(nslot):
        @pl.when(n_tiles % nslot == r)
        def _(r=r):
            probs(n_tiles, r)
            accumulate(n_tiles, r)

    lv = lam_ref[...]
    lam = (jnp.exp(jnp.sum(lv[0:1] * lv[1:2], axis=-1, keepdims=True))
           - jnp.exp(jnp.sum(lv[2:3] * lv[3:4], axis=-1, keepdims=True)) + lam_init)
    a0 = acc_ref[0][...]
    a1 = acc_ref[1][...]
    o0 = a0[:DA_V_DIM] * (1.0 / a0[DA_V_DIM:DA_V_DIM + 1])
    o1 = a1[:DA_V_DIM] * (1.0 / a1[DA_V_DIM:DA_V_DIM + 1])
    out = o0 - lam * o1
    inv = lax.rsqrt(jnp.mean(out * out, axis=0, keepdims=True) + NORM_EPS)
    out = out * inv * gn_ref[...] * (1.0 - lam_init)
    o_ref[...] = out.T.astype(o_ref.dtype)


def _diff_attention(proj, q_t, v_t, kn, lamvec, gn, batch, seq, tq, lam_init):
    t = proj.shape[0]
    nk = seq // tq
    augq, augk, qbias, corr = _attn_consts(tq)
    gnb = jnp.broadcast_to(gn.reshape(DA_V_DIM, 1), (DA_V_DIM, tq))
    nslot = _attn_slots(seq, tq)
    body = functools.partial(_attn_body, tq=tq, nk=nk, nslot=nslot, lam_init=lam_init)
    kblk = PCOL_DA_K // LANES
    return pl.pallas_call(
        body,
        grid=(batch, DA_HEADS, nk),
        in_specs=[
            pl.BlockSpec((8, LANES), lambda b, h, i: (0, 0)),
            pl.BlockSpec((None, LANES, tq), lambda b, h, i: (b, h, i)),
            pl.BlockSpec((seq, LANES), lambda b, h, i: (b, kblk + h)),
            pl.BlockSpec((None, None, DA_V_DIM + ONES_ROWS, seq), lambda b, h, i: (b, h, 0, 0)),
            pl.BlockSpec((kn.shape[0] // batch, 1, LANES), lambda b, h, i: (b, 0, 0)),
            pl.BlockSpec((None, 2, 2, LANES, tq), lambda b, h, i: (h, 0, 0, 0, 0)),
            pl.BlockSpec((2, tq, LANES), lambda b, h, i: (0, 0, 0)),
            pl.BlockSpec((None, 2, 1, tq), lambda b, h, i: (h, 0, 0, 0)),
            pl.BlockSpec((None, tq, tq), lambda b, h, i: (h, 0, 0)),
            pl.BlockSpec((DA_V_DIM, tq), lambda b, h, i: (0, 0)),
        ],
        out_specs=pl.BlockSpec((tq, LANES), lambda b, h, i: (b * nk + i, h)),
        out_shape=jax.ShapeDtypeStruct((t, DA_WIDTH), BF16),
        scratch_shapes=[
            pltpu.VMEM((4, LANES, tq), BF16),
        ] + [pltpu.VMEM((tq, tq), F32)] * (2 * nslot)
        + [pltpu.VMEM((tq, tq), BF16)] * (2 * nslot)
        + [pltpu.VMEM((1, tq), F32)] * (2 * nslot)
        + [pltpu.VMEM((1, tq), F32)] * (2 * nslot)
        + [pltpu.VMEM((DA_V_DIM + ONES_ROWS, tq), F32)] * 2
        + [pltpu.VMEM((1, tq), F32)] * 2,
        compiler_params=pltpu.CompilerParams(
            dimension_semantics=("arbitrary", "arbitrary", "arbitrary"),
            vmem_limit_bytes=VMEM_LIMIT),
        name="diff_attention",
    )(lamvec, q_t, proj, v_t, kn, augq, augk, qbias, corr, gnb)


def _log_sigmoid(z):
    return jnp.minimum(z, 0.0) - jnp.log1p(jnp.exp(-jnp.abs(z)))


def _gla_chunk(q, k, v, g, st_ref, reverse):
    c = GLA_CHUNK
    q = q.astype(F32)
    k = k.astype(F32)
    row = lax.broadcasted_iota(jnp.int32, (c, GLA_QK_WIDTH), 0)
    lane = lax.broadcasted_iota(jnp.int32, (c, GLA_QK_WIDTH), 1)
    b = g
    shift = 1
    while shift < c:
        b = b + jnp.where(row >= shift, pltpu.roll(b, shift, 0), 0.0)
        shift *= 2
    tot = b[c - 1:c, :]
    if reverse:
        b = tot - b + g
    q_t = (q * jnp.exp(b)).astype(BF16)
    k_t = (k * jnp.exp(-b)).astype(BF16)
    k_end = (k * jnp.exp(tot - b)).astype(BF16)
    decay = jnp.exp(tot)

    head = lane // GLA_DK
    zero_k = jnp.zeros_like(k_t)
    k_bd = jnp.concatenate([jnp.where(head == h, k_t, zero_k) for h in range(GLA_HEADS)], axis=0)
    att = _nt_dot(q_t, k_bd)
    pos = lane % c
    keep = (pos > row) if reverse else (pos <= row)
    att = jnp.where(keep, att, 0.0).astype(BF16)

    vhead = lax.broadcasted_iota(jnp.int32, (c, GLA_WIDTH), 1) // GLA_DV
    zero_v = jnp.zeros_like(v)
    v_bd = jnp.concatenate([jnp.where(vhead == h, v, zero_v) for h in range(GLA_HEADS)], axis=0)
    st = st_ref[...]
    st_b = st.astype(BF16)
    shead = lax.broadcasted_iota(jnp.int32, st.shape, 1) // GLA_DK
    zero_s = jnp.zeros_like(st_b)
    st_bd = jnp.concatenate([jnp.where(shead == h, st_b, zero_s) for h in range(GLA_HEADS)], axis=0)
    out = _dot(att, v_bd) + _nt_dot(q_t, st_bd)

    kv_t = _tn_dot(v, k_end)
    upd = kv_t[(GLA_HEADS - 1) * GLA_DV:]
    for h in range(GLA_HEADS - 2, -1, -1):
        upd = jnp.where(shead == h, kv_t[h * GLA_DV:(h + 1) * GLA_DV], upd)
    st_ref[...] = decay * st + upd
    return out


def _gla_body(qf_ref, kf_ref, vf_ref, lrf_ref, qb_ref, kb_ref, vb_ref, lrb_ref, wg_ref, bg_ref,
              of_ref, ob_ref, stf_ref, stb_ref, *, rows):
    @pl.when(pl.program_id(1) == 0)
    def _():
        stf_ref[...] = jnp.zeros(stf_ref.shape, F32)
        stb_ref[...] = jnp.zeros(stb_ref.shape, F32)

    wg = wg_ref[...]
    bg = bg_ref[...]
    zf = _dot(lrf_ref[...], wg[:, :GLA_QK_WIDTH]) + bg[:, :GLA_QK_WIDTH]
    zb = _dot(lrb_ref[...], wg[:, GLA_QK_WIDTH:]) + bg[:, GLA_QK_WIDTH:]
    gf = _log_sigmoid(zf) / GLA_GATE_NORM
    gb = _log_sigmoid(zb) / GLA_GATE_NORM
    nc = rows // GLA_CHUNK
    for ci in range(nc):
        sf = slice(ci * GLA_CHUNK, (ci + 1) * GLA_CHUNK)
        of_ref[sf, :] = _gla_chunk(qf_ref[sf, :], kf_ref[sf, :], vf_ref[sf, :], gf[sf, :],
                                   stf_ref, False)
        cj = nc - 1 - ci
        sb = slice(cj * GLA_CHUNK, (cj + 1) * GLA_CHUNK)
        ob_ref[sb, :] = _gla_chunk(qb_ref[sb, :], kb_ref[sb, :], vb_ref[sb, :], gb[sb, :],
                                   stb_ref, True)


def _gla(proj, lr, wg, bg, batch, seq, rows):
    t = proj.shape[0]
    nb = seq // rows
    qblk = PCOL_GQ // GLA_QK_WIDTH
    kblk = PCOL_GK // GLA_QK_WIDTH
    vblk = PCOL_GV // GLA_WIDTH

    def fwd(col):
        return lambda b, i: (b * nb + i, col)

    def bwd(col):
        return lambda b, i: (b * nb + nb - 1 - i, col)

    def specs(ix):
        return [
            pl.BlockSpec((rows, GLA_QK_WIDTH), ix(qblk)),
            pl.BlockSpec((rows, GLA_QK_WIDTH), ix(kblk)),
            pl.BlockSpec((rows, GLA_WIDTH), ix(vblk)),
            pl.BlockSpec((rows, LANES), ix(0)),
        ]

    return pl.pallas_call(
        functools.partial(_gla_body, rows=rows),
        grid=(batch, nb),
        in_specs=specs(fwd) + specs(bwd) + [
            pl.BlockSpec((LANES, 2 * GLA_QK_WIDTH), lambda b, i: (0, 0)),
            pl.BlockSpec((1, 2 * GLA_QK_WIDTH), lambda b, i: (0, 0)),
        ],
        out_specs=[
            pl.BlockSpec((rows, GLA_WIDTH), fwd(0)),
            pl.BlockSpec((rows, GLA_WIDTH), bwd(0)),
        ],
        out_shape=[jax.ShapeDtypeStruct((t, GLA_WIDTH), F32)] * 2,
        scratch_shapes=[pltpu.VMEM((GLA_DV, GLA_QK_WIDTH), F32)] * 2,
        compiler_params=pltpu.CompilerParams(
            dimension_semantics=("arbitrary", "arbitrary"), vmem_limit_bytes=VMEM_LIMIT),
        name="gla",
    )(proj, proj, proj, lr, proj, proj, proj, lr, wg, bg)


def _out_proj_body(da_ref, of_ref, ob_ref, gr_ref, x_ref, gng_ref, wout_ref, pmg_ref, pfg_ref,
                   x1_ref, h2_ref):
    o = of_ref[...] + ob_ref[...]
    gr = gr_ref[...].astype(F32)
    parts = []
    for h in range(GLA_HEADS):
        sl = slice(h * GLA_DV, (h + 1) * GLA_DV)
        gh = gr[:, sl]
        parts.append(_rms(o[:, sl], gng_ref[...]) * (gh * jax.nn.sigmoid(gh)))
    gla = jnp.concatenate(parts, axis=1).astype(BF16)
    mix = _dot(jnp.concatenate([da_ref[...], gla], axis=1), wout_ref[...])
    x1 = x_ref[...] + _rms(mix, pmg_ref[...])
    x1_ref[...] = x1
    h2_ref[...] = _rms(x1, pfg_ref[...]).astype(BF16)


def _out_proj(da, o_f, o_b, proj, x2d, gng, w_out, pmg, pfg, tm):
    t = x2d.shape[0]
    row = lambda i: (i, 0)
    const = lambda i: (0, 0)
    return pl.pallas_call(
        _out_proj_body,
        grid=(t // tm,),
        in_specs=[
            pl.BlockSpec((tm, DA_WIDTH), row),
            pl.BlockSpec((tm, GLA_WIDTH), row),
            pl.BlockSpec((tm, GLA_WIDTH), row),
            pl.BlockSpec((tm, GLA_WIDTH), lambda i: (i, PCOL_GR // GLA_WIDTH)),
            pl.BlockSpec((tm, D_MODEL), row),
            pl.BlockSpec((1, GLA_DV), const),
            pl.BlockSpec((D_MODEL, D_MODEL), const),
            pl.BlockSpec((1, D_MODEL), const),
            pl.BlockSpec((1, D_MODEL), const),
        ],
        out_specs=[pl.BlockSpec((tm, D_MODEL), row), pl.BlockSpec((tm, D_MODEL), row)],
        out_shape=[jax.ShapeDtypeStruct((t, D_MODEL), F32),
                   jax.ShapeDtypeStruct((t, D_MODEL), BF16)],
        compiler_params=pltpu.CompilerParams(
            dimension_semantics=("arbitrary",), vmem_limit_bytes=VMEM_LIMIT),
        name="out_proj",
    )(da, o_f, o_b, proj, x2d, gng, w_out, pmg, pfg)


HALO = 16


def _ffn_body(h_ref, hp_ref, hn_ref, wg_ref, wu_ref, cw_ref, cb_ref, wd_ref, x1_ref, p_ref,
              pfg_ref, wpg_ref, bpg_ref, wpp_ref, png_ref, o_ref, hext_ref, act_ref,
              *, tm, tf, seq, nff):
    i = pl.program_id(0)
    j = pl.program_id(1)

    @pl.when(j == 0)
    def _():
        first = (i * tm) % seq == 0
        last = ((i + 1) * tm) % seq == 0
        hp = hp_ref[...]
        hn = hn_ref[...]
        hext_ref[0:HALO, :] = jnp.where(first, jnp.zeros_like(hp), hp)
        hext_ref[HALO:HALO + tm, :] = h_ref[...]
        hext_ref[HALO + tm:, :] = jnp.where(last, jnp.zeros_like(hn), hn)

    gate = _dot(hext_ref[...], wg_ref[...])
    ext = tm + 2 * HALO
    cw = cw_ref[...]
    conv = (pltpu.roll(gate, 1, 0)[HALO:HALO + tm] * cw[0:1]
            + gate[HALO:HALO + tm] * cw[1:2]
            + pltpu.roll(gate, ext - 1, 0)[HALO:HALO + tm] * cw[2:3]
            + cb_ref[...])
    up = _dot(h_ref[...], wu_ref[...])
    act_ref[:, pl.ds(pl.multiple_of(j * tf, tf), tf)] = (jax.nn.gelu(conv) * up).astype(BF16)

    @pl.when(j == nff - 1)
    def _():
        x2 = x1_ref[...] + _rms(_dot(act_ref[...], wd_ref[...]), pfg_ref[...])
        e = _rms(_dot(p_ref[...].astype(BF16), wpp_ref[...]), png_ref[...])
        gate_e = jax.nn.sigmoid(_dot(x2.astype(BF16), wpg_ref[...]) + bpg_ref[...])
        o_ref[...] = x2 + gate_e * e


def _ffn(h2, x1, p2d, w_up, conv_w, conv_b, w_down, pfg, w_pg, b_pg, w_pp, png, seq, tm, tf):
    t = h2.shape[0]
    nff = D_FF // tf
    nhalo = t // HALO
    per = tm // HALO
    row = lambda i, j: (i, 0)
    const = lambda i, j: (0, 0)
    return pl.pallas_call(
        functools.partial(_ffn_body, tm=tm, tf=tf, seq=seq, nff=nff),
        grid=(t // tm, nff),
        in_specs=[
            pl.BlockSpec((tm, D_MODEL), row),
            pl.BlockSpec((HALO, D_MODEL), lambda i, j: (jnp.maximum(i * per - 1, 0), 0)),
            pl.BlockSpec((HALO, D_MODEL), lambda i, j: (jnp.minimum((i + 1) * per, nhalo - 1), 0)),
            pl.BlockSpec((D_MODEL, tf), lambda i, j: (0, j)),
            pl.BlockSpec((D_MODEL, tf), lambda i, j: (0, nff + j)),
            pl.BlockSpec((3, tf), lambda i, j: (0, j)),
            pl.BlockSpec((1, tf), lambda i, j: (0, j)),
            pl.BlockSpec((D_FF, D_MODEL), const, pipeline_mode=pl.Buffered(1)),
            pl.BlockSpec((tm, D_MODEL), row),
            pl.BlockSpec((tm, PLE_DIM), row),
            pl.BlockSpec((1, D_MODEL), const),
            pl.BlockSpec((D_MODEL, D_MODEL), const, pipeline_mode=pl.Buffered(1)),
            pl.BlockSpec((1, D_MODEL), const),
            pl.BlockSpec((PLE_DIM, D_MODEL), const, pipeline_mode=pl.Buffered(1)),
            pl.BlockSpec((1, D_MODEL), const),
        ],
        out_specs=pl.BlockSpec((tm, D_MODEL), row),
        out_shape=jax.ShapeDtypeStruct((t, D_MODEL), F32),
        scratch_shapes=[
            pltpu.VMEM((tm + 2 * HALO, D_MODEL), BF16),
            pltpu.VMEM((tm, D_FF), BF16),
        ],
        compiler_params=pltpu.CompilerParams(
            dimension_semantics=("arbitrary", "arbitrary"), vmem_limit_bytes=VMEM_LIMIT),
        name="conv_ffn",
    )(h2, h2, h2, w_up, w_up, conv_w, conv_b, w_down, x1, p2d, pfg, w_pg, b_pg, w_pp, png)


def _prep_weights(l, w_in, gla_w_gate_f, gla_b_gate_f, gla_w_gate_b, gla_b_gate_b, da_lq1,
                  da_lk1, da_lq2, da_lk2):
    w = w_in[l]
    qscale = jnp.concatenate([
        jnp.full((DA_WIDTH,), DA_QK_DIM ** -0.5 * LOG2E, F32), jnp.ones((2 * DA_WIDTH,), F32),
        jnp.full((GLA_QK_WIDTH,), GLA_DK ** -0.5, F32),
        jnp.ones((GLA_QK_WIDTH + 2 * GLA_WIDTH,), F32)])
    w_main = (w[:, :MAIN_WIDTH] * qscale).astype(BF16)
    w_lr = jnp.zeros((D_MODEL, LANES), F32).at[:, :2 * GLA_GATE_RANK].set(w[:, MAIN_WIDTH:])
    wg = jnp.zeros((LANES, 2 * GLA_QK_WIDTH), F32)
    wg = wg.at[:GLA_GATE_RANK, :GLA_QK_WIDTH].set(gla_w_gate_f[l])
    wg = wg.at[GLA_GATE_RANK:2 * GLA_GATE_RANK, GLA_QK_WIDTH:].set(gla_w_gate_b[l])
    bg = jnp.concatenate([gla_b_gate_f[l], gla_b_gate_b[l]])[None, :]
    lamvec = jnp.zeros((8, LANES), F32)
    for r, vec in enumerate((da_lq1, da_lk1, da_lq2, da_lk2)):
        lamvec = lamvec.at[r, :DA_QK_DIM].set(vec[l].astype(F32))
    return w_main, w_lr.astype(BF16), wg.astype(BF16), bg, lamvec


ROW_TILE = 512
ATTN_TILE = 1024
GLA_ROWS = 512
FF_TILE = 2048


def _attn_slots(seq, tile):
    return min(NSLOT, max(2, seq // tile))


def _attn_tile(seq):
    tile = min(seq, ATTN_TILE)
    while tile > LANES:
        resident = 2 * 2 * seq * (LANES + DA_V_DIM + ONES_ROWS)
        tiles = tile * tile * (2 * _attn_slots(seq, tile) * (4 + 2) + 2 * 4)
        if resident + tiles <= VMEM_LIMIT * 3 // 4:
            break
        tile //= 2
    return tile


def _layer(x2d, p2d, batch, seq, lam_init, wts):
    tm = min(seq, ROW_TILE)
    proj, q_t, v_t, lr, kn = _in_proj(x2d, wts["pre_mix_g"], wts["w_main"], wts["w_lr"], batch, seq, tm)
    da = _diff_attention(proj, q_t, v_t, kn, wts["lamvec"], wts["da_norm_g"], batch, seq,
                         _attn_tile(seq), lam_init)
    o_f, o_b = _gla(proj, lr, wts["wg"], wts["bg"], batch, seq, min(seq, GLA_ROWS))
    x1, h2 = _out_proj(da, o_f, o_b, proj, x2d, wts["gla_norm_g"], wts["w_out"],
                       wts["post_mix_g"], wts["pre_ffn_g"], tm)
    return _ffn(h2, x1, p2d, wts["w_ffn_up"], wts["ffn_conv_w"], wts["ffn_conv_b"],
                wts["w_ffn_down"], wts["post_ffn_g"], wts["w_ple_gate"], wts["b_ple_gate"],
                wts["w_ple_proj"], wts["ple_norm_g"], seq, tm, FF_TILE)


def kernel(x_prompt, x_sample, p_prompt, p_sample, pre_mix_g, w_in, da_lq1, da_lk1, da_lq2, da_lk2, da_norm_g, gla_w_gate_f, gla_b_gate_f, gla_w_gate_b, gla_b_gate_b, gla_norm_g, w_out, post_mix_g, pre_ffn_g, w_ffn_up, ffn_conv_w, ffn_conv_b, w_ffn_down, post_ffn_g, w_ple_gate, b_ple_gate, w_ple_proj, ple_norm_g):
    depth = w_in.shape[0]
    layers = []
    for l in range(depth):
        w_main, w_lr, wg, bg, lamvec = _prep_weights(
            l, w_in, gla_w_gate_f, gla_b_gate_f, gla_w_gate_b, gla_b_gate_b,
            da_lq1, da_lk1, da_lq2, da_lk2)
        row = lambda a: a[l][None, :].astype(F32)
        layers.append(dict(
            w_main=w_main, w_lr=w_lr, wg=wg, bg=bg, lamvec=lamvec,
            pre_mix_g=row(pre_mix_g), da_norm_g=row(da_norm_g), gla_norm_g=row(gla_norm_g),
            w_out=w_out[l].astype(BF16), post_mix_g=row(post_mix_g), pre_ffn_g=row(pre_ffn_g),
            w_ffn_up=w_ffn_up[l].astype(BF16), ffn_conv_w=ffn_conv_w[l].astype(F32),
            ffn_conv_b=row(ffn_conv_b), w_ffn_down=w_ffn_down[l].astype(BF16),
            post_ffn_g=row(post_ffn_g), w_ple_gate=w_ple_gate[l].astype(BF16),
            b_ple_gate=row(b_ple_gate), w_ple_proj=w_ple_proj[l].astype(BF16),
            ple_norm_g=row(ple_norm_g)))

    def trunk(x, p):
        batch, seq, _ = x.shape
        t = batch * seq
        x2d = x.reshape(t, D_MODEL)
        for l in range(depth):
            lam_init = 0.8 - 0.6 * math.exp(-0.3 * l)
            x2d = _layer(x2d, p[l].reshape(t, PLE_DIM), batch, seq, lam_init, layers[l])
        return x2d.reshape(batch, seq, D_MODEL)

    return (trunk(x_prompt, p_prompt), trunk(x_sample, p_sample))
```

```python
import functools
import math

import jax
import jax.numpy as jnp
from jax import lax
from jax.experimental import pallas as pl
from jax.experimental.pallas import tpu as pltpu

F32 = jnp.float32
BF16 = jnp.bfloat16

D_MODEL = 1024
PLE_DIM = 256
DA_HEADS = 4
DA_QK_DIM = 64
DA_V_DIM = 128
DA_WIDTH = DA_HEADS * DA_V_DIM
GLA_HEADS = 4
GLA_DK = 64
GLA_DV = 128
GLA_QK_WIDTH = GLA_HEADS * GLA_DK
GLA_WIDTH = GLA_HEADS * GLA_DV
GLA_GATE_RANK = 16
GLA_GATE_NORM = 16.0
GLA_CHUNK = 64
D_FF = 4 * D_MODEL
NORM_EPS = 1e-6
LANES = 128
MAIN_WIDTH = 3 * DA_WIDTH + 2 * GLA_QK_WIDTH + 2 * GLA_WIDTH
NEG_BIG = -1e30
LOG2E = math.log2(math.e)
ZERO_PROB_GAP = 138.0
ONES_ROWS = 16
NSLOT = 2
V7X_VMEM_BYTES = 64 * 1024 * 1024
VMEM_LIMIT = V7X_VMEM_BYTES * 7 // 8

COL_DA_Q = 0
COL_DA_K = DA_WIDTH
COL_DA_V = 2 * DA_WIDTH
COL_GLA = 3 * DA_WIDTH
PROJ_WIDTH = DA_WIDTH + 2 * GLA_QK_WIDTH + 2 * GLA_WIDTH
PCOL_DA_K = 0
PCOL_GQ = DA_WIDTH
PCOL_GK = PCOL_GQ + GLA_QK_WIDTH
PCOL_GV = PCOL_GK + GLA_QK_WIDTH
PCOL_GR = PCOL_GV + GLA_WIDTH


def _rms(x, g):
    return x * lax.rsqrt(jnp.mean(x * x, axis=-1, keepdims=True) + NORM_EPS) * g


def _nt_dot(a, b):
    return lax.dot_general(a, b, (((1,), (1,)), ((), ())), preferred_element_type=F32)


def _tn_dot(a, b):
    return lax.dot_general(a, b, (((0,), (0,)), ((), ())), preferred_element_type=F32)


def _dot(a, b):
    return jnp.dot(a, b, preferred_element_type=F32)


def _in_proj_body(x_ref, g_ref, wm_ref, wlr_ref, gsel_ref, proj_ref, qt_ref, vt_ref, lr_ref, kn_ref):
    h = _rms(x_ref[...], g_ref[...]).astype(BF16)
    main = _dot(h, wm_ref[...])
    proj_ref[:, :DA_WIDTH] = main[:, COL_DA_K:COL_DA_K + DA_WIDTH].astype(BF16)
    proj_ref[:, DA_WIDTH:] = main[:, COL_GLA:].astype(BF16)
    qt_ref[...] = main[:, COL_DA_Q:COL_DA_Q + DA_WIDTH].T.astype(BF16)
    for hd in range(DA_HEADS):
        c0 = COL_DA_V + hd * DA_V_DIM
        vt_ref[hd, :DA_V_DIM, :] = main[:, c0:c0 + DA_V_DIM].T.astype(BF16)
        vt_ref[hd, DA_V_DIM:, :] = jnp.ones((ONES_ROWS, main.shape[0]), BF16)
    lr_ref[...] = _dot(h, wlr_ref[...]).astype(BF16)
    k = main[:, COL_DA_K:COL_DA_K + DA_WIDTH]
    kn_ref[...] = jnp.max(_dot((k * k).astype(BF16), gsel_ref[...]), axis=0, keepdims=True)


def _in_proj(x2d, g, w_main, w_lr, batch, seq, tm):
    t = x2d.shape[0]
    per_seq = seq // tm
    va_rows = DA_V_DIM + ONES_ROWS
    col_group = jnp.arange(DA_WIDTH, dtype=jnp.int32)[:, None] // DA_QK_DIM
    gsel = (col_group == jnp.arange(LANES, dtype=jnp.int32)[None, :]).astype(BF16)
    return pl.pallas_call(
        _in_proj_body,
        grid=(t // tm,),
        in_specs=[
            pl.BlockSpec((tm, D_MODEL), lambda i: (i, 0)),
            pl.BlockSpec((1, D_MODEL), lambda i: (0, 0)),
            pl.BlockSpec((D_MODEL, MAIN_WIDTH), lambda i: (0, 0)),
            pl.BlockSpec((D_MODEL, LANES), lambda i: (0, 0)),
            pl.BlockSpec((DA_WIDTH, LANES), lambda i: (0, 0)),
        ],
        out_specs=[
            pl.BlockSpec((tm, PROJ_WIDTH), lambda i: (i, 0)),
            pl.BlockSpec((None, DA_WIDTH, tm), lambda i: (i // per_seq, 0, i % per_seq)),
            pl.BlockSpec((None, DA_HEADS, va_rows, tm),
                         lambda i: (i // per_seq, 0, 0, i % per_seq)),
            pl.BlockSpec((tm, LANES), lambda i: (i, 0)),
            pl.BlockSpec((None, 1, LANES), lambda i: (i, 0, 0)),
        ],
        out_shape=[
            jax.ShapeDtypeStruct((t, PROJ_WIDTH), BF16),
            jax.ShapeDtypeStruct((batch, DA_WIDTH, seq), BF16),
            jax.ShapeDtypeStruct((batch, DA_HEADS, va_rows, seq), BF16),
            jax.ShapeDtypeStruct((t, LANES), BF16),
            jax.ShapeDtypeStruct((t // tm, 1, LANES), F32),
        ],
        compiler_params=pltpu.CompilerParams(
            dimension_semantics=("arbitrary",), vmem_limit_bytes=VMEM_LIMIT),
        name="in_proj",
    )(x2d, g, w_main, w_lr, gsel)


def _split3(x):
    hi = x.astype(BF16).astype(F32)
    mid = (x - hi).astype(BF16).astype(F32)
    lo = (x - hi - mid).astype(BF16).astype(F32)
    return hi, mid, lo


def _attn_consts(tq):
    i = jnp.arange(tq, dtype=jnp.int32)
    lo = (i & 255).astype(F32)
    hi = (i - (i & 255)).astype(F32)
    slopes = jnp.asarray([2.0 ** (-8.0 * (h + 1) / DA_HEADS) for h in range(DA_HEADS)], F32) * LOG2E
    pieces = jnp.stack([p for piece in _split3(slopes) for p in (piece, piece)], axis=1)
    half = jnp.zeros((DA_HEADS, DA_QK_DIM, tq), F32).at[:, :6].set(
        jnp.broadcast_to(pieces[:, :, None], (DA_HEADS, 6, tq)))
    zeros = jnp.zeros_like(half)
    left = jnp.stack([jnp.concatenate([zeros, half], 1),
                      jnp.concatenate([half, zeros], 1)], axis=1)
    augq = jnp.stack([left, -left], axis=1).astype(BF16)
    kcols = jnp.stack([lo, hi, lo, hi, lo, hi], axis=-1)
    khalf = jnp.zeros((tq, DA_QK_DIM), F32).at[:, :6].set(kcols)
    kz = jnp.zeros_like(khalf)
    augk = jnp.stack([jnp.concatenate([kz, khalf], -1),
                      jnp.concatenate([khalf, kz], -1)], axis=0).astype(BF16)
    qb = slopes[:, None] * i.astype(F32)[None, :]
    qbias = jnp.stack([-qb, qb], axis=1)[:, :, None, :]
    d = (i[None, :] - i[:, None]).astype(F32)
    corr = 2.0 * slopes[:, None, None] * jnp.minimum(d, 0.0)[None]
    return augq, augk, qbias, corr


def _attn_body(lam_ref, qt_ref, k_ref, vt_ref, kn_ref, augq_ref, augk_ref, qbias_ref, corr_ref, gn_ref,
               o_ref, qa_ref, *scratch, tq, nk, nslot, lam_init):
    grab = lambda g: [scratch[2 * (g * nslot + sl):2 * (g * nslot + sl) + 2] for sl in range(nslot)]
    s_ref, p_ref, cmax_ref, alpha_ref = grab(0), grab(1), grab(2), grab(3)
    acc_ref, m_ref = scratch[8 * nslot:8 * nslot + 2], scratch[8 * nslot + 2:8 * nslot + 4]
    h = pl.program_id(1)
    qi = pl.program_id(2)
    slope = jnp.where(h == 0, 2.0 ** -2, jnp.where(h == 1, 2.0 ** -4,
                      jnp.where(h == 2, 2.0 ** -6, 2.0 ** -8))).astype(F32) * LOG2E

    qt = qt_ref[...]
    row = lax.broadcasted_iota(jnp.int32, (LANES, tq), 0)
    for side in range(2):
        qa_ref[2 * side] = jnp.where(row < DA_QK_DIM, qt, augq_ref[side, 0])
        qa_ref[2 * side + 1] = jnp.where(row >= DA_QK_DIM, qt, augq_ref[side, 1])
    for mp in range(2):
        m_ref[mp][...] = jnp.full(m_ref[mp].shape, NEG_BIG, F32)
        acc_ref[mp][...] = jnp.zeros(acc_ref[mp].shape, F32)
    lane = lax.broadcasted_iota(jnp.int32, (tq, LANES), 1)

    def scores(kt, side, slot, diag):
        k = k_ref[pl.ds(pl.multiple_of(kt * tq, tq), tq), :]
        ka = (jnp.where(lane < DA_QK_DIM, k, augk_ref[0]),
              jnp.where(lane >= DA_QK_DIM, k, augk_ref[1]))
        for mp in range(2):
            s = _dot(ka[mp], qa_ref[2 * side + mp])
            if diag:
                s = s + corr_ref[...]
            s_ref[slot][mp][...] = s
            cmax_ref[slot][mp][...] = jnp.max(s, axis=0, keepdims=True) + qbias_ref[side]

    scores(qi, 0, 0, True)

    qf = qt.astype(F32)
    lane1 = lax.broadcasted_iota(jnp.int32, (1, LANES), 1)
    kn2 = jnp.max(kn_ref[...], axis=0)
    gap = None
    for mp in range(2):
        rows = slice(mp * DA_QK_DIM, (mp + 1) * DA_QK_DIM)
        qn2 = jnp.sum(qf[rows] * qf[rows], axis=0, keepdims=True)
        kn2_mp = jnp.max(jnp.where(lane1 == 2 * h + mp, kn2, 0.0), axis=1, keepdims=True)
        g = jnp.max(1.02 * jnp.sqrt(qn2 * kn2_mp) - cmax_ref[0][mp][...], axis=1, keepdims=True)
        gap = g if gap is None else jnp.maximum(gap, g)
    far = ((gap + ZERO_PROB_GAP) / slope - 1.0) / tq
    reach = jnp.where(far < nk, jnp.maximum(jnp.ceil(far), 0.0), float(nk)).astype(jnp.int32)[0, 0]
    n_left = jnp.minimum(qi, reach)
    n_tiles = n_left + jnp.minimum(nk - 1 - qi, reach)

    def key_tile(t):
        return jnp.where(t <= n_left, qi - t, qi + (t - n_left))

    def tile_side(t):
        return (t > n_left).astype(jnp.int32)

    def tile_const(t):
        return -slope * (jnp.abs(qi - key_tile(t)) * tq).astype(F32)

    def probs(t, slot):
        shift = tile_const(t) + qbias_ref[tile_side(t)]
        for mp in range(2):
            m_old = m_ref[mp][...]
            m_new = jnp.maximum(m_old, cmax_ref[slot][mp][...] + tile_const(t))
            alpha_ref[slot][mp][...] = jnp.exp2(m_old - m_new)
            p_ref[slot][mp][...] = jnp.exp2(s_ref[slot][mp][...] - (m_new - shift)).astype(BF16)
            m_ref[mp][...] = m_new

    def accumulate(t, slot):
        va = vt_ref[:, pl.ds(pl.multiple_of(key_tile(t) * tq, tq), tq)]
        for mp in range(2):
            acc_ref[mp][...] = (alpha_ref[slot][mp][...] * acc_ref[mp][...]
                                + _dot(va, p_ref[slot][mp][...]))

    def step(t, slot):
        prev = (slot - 1) % nslot

        @pl.when(t <= n_tiles)
        def _():
            probs(t - 1, prev)
            scores(key_tile(t), tile_side(t), slot, False)
            accumulate(t - 1, prev)

    def trip(i, carry):
        for r in range(1, nslot + 1):
            step(nslot * i + r, r % nslot)
        return carry

    lax.fori_loop(0, (n_tiles + nslot - 1) // nslot, trip, 0)
    for r in range(nslot):
        @pl.when(n_tiles % nslot == r)
        def _(r=r):
            probs(n_tiles, r)
            accumulate(n_tiles, r)

    lv = lam_ref[...]
    lam = (jnp.exp(jnp.sum(lv[0:1] * lv[1:2], axis=-1, keepdims=True))
           - jnp.exp(jnp.sum(lv[2:3] * lv[3:4], axis=-1, keepdims=True)) + lam_init)
    a0 = acc_ref[0][...]
    a1 = acc_ref[1][...]
    o0 = a0[:DA_V_DIM] * (1.0 / a0[DA_V_DIM:DA_V_DIM + 1])
    o1 = a1[:DA_V_DIM] * (1.0 / a1[DA_V_DIM:DA_V_DIM + 1])
    out = o0 - lam * o1
    inv = lax.rsqrt(jnp.mean(out * out, axis=0, keepdims=True) + NORM_EPS)
    out = out * inv * gn_ref[...] * (1.0 - lam_init)
    o_ref[...] = out.T.astype(o_ref.dtype)


def _diff_attention(proj, q_t, v_t, kn, lamvec, gn, batch, seq, tq, lam_init):
    t = proj.shape[0]
    nk = seq // tq
    augq, augk, qbias, corr = _attn_consts(tq)
    gnb = jnp.broadcast_to(gn.reshape(DA_V_DIM, 1), (DA_V_DIM, tq))
    nslot = NSLOT
    body = functools.partial(_attn_body, tq=tq, nk=nk, nslot=nslot, lam_init=lam_init)
    kblk = PCOL_DA_K // LANES
    return pl.pallas_call(
        body,
        grid=(batch, DA_HEADS, nk),
        in_specs=[
            pl.BlockSpec((8, LANES), lambda b, h, i: (0, 0)),
            pl.BlockSpec((None, LANES, tq), lambda b, h, i: (b, h, i)),
            pl.BlockSpec((seq, LANES), lambda b, h, i: (b, kblk + h)),
            pl.BlockSpec((None, None, DA_V_DIM + ONES_ROWS, seq), lambda b, h, i: (b, h, 0, 0)),
            pl.BlockSpec((kn.shape[0] // batch, 1, LANES), lambda b, h, i: (b, 0, 0)),
            pl.BlockSpec((None, 2, 2, LANES, tq), lambda b, h, i: (h, 0, 0, 0, 0)),
            pl.BlockSpec((2, tq, LANES), lambda b, h, i: (0, 0, 0)),
            pl.BlockSpec((None, 2, 1, tq), lambda b, h, i: (h, 0, 0, 0)),
            pl.BlockSpec((None, tq, tq), lambda b, h, i: (h, 0, 0)),
            pl.BlockSpec((DA_V_DIM, tq), lambda b, h, i: (0, 0)),
        ],
        out_specs=pl.BlockSpec((tq, LANES), lambda b, h, i: (b * nk + i, h)),
        out_shape=jax.ShapeDtypeStruct((t, DA_WIDTH), BF16),
        scratch_shapes=[
            pltpu.VMEM((4, LANES, tq), BF16),
        ] + [pltpu.VMEM((tq, tq), F32)] * (2 * nslot)
        + [pltpu.VMEM((tq, tq), BF16)] * (2 * nslot)
        + [pltpu.VMEM((1, tq), F32)] * (2 * nslot)
        + [pltpu.VMEM((1, tq), F32)] * (2 * nslot)
        + [pltpu.VMEM((DA_V_DIM + ONES_ROWS, tq), F32)] * 2
        + [pltpu.VMEM((1, tq), F32)] * 2,
        compiler_params=pltpu.CompilerParams(
            dimension_semantics=("arbitrary", "arbitrary", "arbitrary"),
            vmem_limit_bytes=VMEM_LIMIT),
        name="diff_attention",
    )(lamvec, q_t, proj, v_t, kn, augq, augk, qbias, corr, gnb)


def _log_sigmoid(z):
    return jnp.minimum(z, 0.0) - jnp.log1p(jnp.exp(-jnp.abs(z)))


def _gla_chunk(q, k, v, g, st_ref, reverse):
    c = GLA_CHUNK
    q = q.astype(F32)
    k = k.astype(F32)
    row = lax.broadcasted_iota(jnp.int32, (c, GLA_QK_WIDTH), 0)
    lane = lax.broadcasted_iota(jnp.int32, (c, GLA_QK_WIDTH), 1)
    b = g
    shift = 1
    while shift < c:
        b = b + jnp.where(row >= shift, pltpu.roll(b, shift, 0), 0.0)
        shift *= 2
    tot = b[c - 1:c, :]
    if reverse:
        b = tot - b + g
    q_t = (q * jnp.exp(b)).astype(BF16)
    k_t = (k * jnp.exp(-b)).astype(BF16)
    k_end = (k * jnp.exp(tot - b)).astype(BF16)
    decay = jnp.exp(tot)

    head = lane // GLA_DK
    zero_k = jnp.zeros_like(k_t)
    k_bd = jnp.concatenate([jnp.where(head == h, k_t, zero_k) for h in range(GLA_HEADS)], axis=0)
    att = _nt_dot(q_t, k_bd)
    pos = lane % c
    keep = (pos > row) if reverse else (pos <= row)
    att = jnp.where(keep, att, 0.0).astype(BF16)

    vhead = lax.broadcasted_iota(jnp.int32, (c, GLA_WIDTH), 1) // GLA_DV
    zero_v = jnp.zeros_like(v)
    v_bd = jnp.concatenate([jnp.where(vhead == h, v, zero_v) for h in range(GLA_HEADS)], axis=0)
    st = st_ref[...]
    st_b = st.astype(BF16)
    shead = lax.broadcasted_iota(jnp.int32, st.shape, 1) // GLA_DK
    zero_s = jnp.zeros_like(st_b)
    st_bd = jnp.concatenate([jnp.where(shead == h, st_b, zero_s) for h in range(GLA_HEADS)], axis=0)
    out = _dot(att, v_bd) + _nt_dot(q_t, st_bd)

    kv_t = _tn_dot(v, k_end)
    upd = kv_t[(GLA_HEADS - 1) * GLA_DV:]
    for h in range(GLA_HEADS - 2, -1, -1):
        upd = jnp.where(shead == h, kv_t[h * GLA_DV:(h + 1) * GLA_DV], upd)
    st_ref[...] = decay * st + upd
    return out


def _gla_body(qf_ref, kf_ref, vf_ref, lrf_ref, qb_ref, kb_ref, vb_ref, lrb_ref, wg_ref, bg_ref,
              of_ref, ob_ref, stf_ref, stb_ref, *, rows):
    @pl.when(pl.program_id(1) == 0)
    def _():
        stf_ref[...] = jnp.zeros(stf_ref.shape, F32)
        stb_ref[...] = jnp.zeros(stb_ref.shape, F32)

    wg = wg_ref[...]
    bg = bg_ref[...]
    zf = _dot(lrf_ref[...], wg[:, :GLA_QK_WIDTH]) + bg[:, :GLA_QK_WIDTH]
    zb = _dot(lrb_ref[...], wg[:, GLA_QK_WIDTH:]) + bg[:, GLA_QK_WIDTH:]
    gf = _log_sigmoid(zf) / GLA_GATE_NORM
    gb = _log_sigmoid(zb) / GLA_GATE_NORM
    nc = rows // GLA_CHUNK
    for ci in range(nc):
        sf = slice(ci * GLA_CHUNK, (ci + 1) * GLA_CHUNK)
        of_ref[sf, :] = _gla_chunk(qf_ref[sf, :], kf_ref[sf, :], vf_ref[sf, :], gf[sf, :],
                                   stf_ref, False)
        cj = nc - 1 - ci
        sb = slice(cj * GLA_CHUNK, (cj + 1) * GLA_CHUNK)
        ob_ref[sb, :] = _gla_chunk(qb_ref[sb, :], kb_ref[sb, :], vb_ref[sb, :], gb[sb, :],
                                   stb_ref, True)


def _gla(proj, lr, wg, bg, batch, seq, rows):
    t = proj.shape[0]
    nb = seq // rows
    qblk = PCOL_GQ // GLA_QK_WIDTH
    kblk = PCOL_GK // GLA_QK_WIDTH
    vblk = PCOL_GV // GLA_WIDTH

    def fwd(col):
        return lambda b, i: (b * nb + i, col)

    def bwd(col):
        return lambda b, i: (b * nb + nb - 1 - i, col)

    def specs(ix):
        return [
            pl.BlockSpec((rows, GLA_QK_WIDTH), ix(qblk)),
            pl.BlockSpec((rows, GLA_QK_WIDTH), ix(kblk)),
            pl.BlockSpec((rows, GLA_WIDTH), ix(vblk)),
            pl.BlockSpec((rows, LANES), ix(0)),
        ]

    return pl.pallas_call(
        functools.partial(_gla_body, rows=rows),
        grid=(batch, nb),
        in_specs=specs(fwd) + specs(bwd) + [
            pl.BlockSpec((LANES, 2 * GLA_QK_WIDTH), lambda b, i: (0, 0)),
            pl.BlockSpec((1, 2 * GLA_QK_WIDTH), lambda b, i: (0, 0)),
        ],
        out_specs=[
            pl.BlockSpec((rows, GLA_WIDTH), fwd(0)),
            pl.BlockSpec((rows, GLA_WIDTH), bwd(0)),
        ],
        out_shape=[jax.ShapeDtypeStruct((t, GLA_WIDTH), F32)] * 2,
        scratch_shapes=[pltpu.VMEM((GLA_DV, GLA_QK_WIDTH), F32)] * 2,
        compiler_params=pltpu.CompilerParams(
            dimension_semantics=("arbitrary", "arbitrary"), vmem_limit_bytes=VMEM_LIMIT),
        name="gla",
    )(proj, proj, proj, lr, proj, proj, proj, lr, wg, bg)


def _out_proj_body(da_ref, of_ref, ob_ref, gr_ref, x_ref, gng_ref, wout_ref, pmg_ref, pfg_ref,
                   x1_ref, h2_ref):
    o = of_ref[...] + ob_ref[...]
    gr = gr_ref[...].astype(F32)
    parts = []
    for h in range(GLA_HEADS):
        sl = slice(h * GLA_DV, (h + 1) * GLA_DV)
        gh = gr[:, sl]
        parts.append(_rms(o[:, sl], gng_ref[...]) * (gh * jax.nn.sigmoid(gh)))
    gla = jnp.concatenate(parts, axis=1).astype(BF16)
    mix = _dot(jnp.concatenate([da_ref[...], gla], axis=1), wout_ref[...])
    x1 = x_ref[...] + _rms(mix, pmg_ref[...])
    x1_ref[...] = x1
    h2_ref[...] = _rms(x1, pfg_ref[...]).astype(BF16)


def _out_proj(da, o_f, o_b, proj, x2d, gng, w_out, pmg, pfg, tm):
    t = x2d.shape[0]
    row = lambda i: (i, 0)
    const = lambda i: (0, 0)
    return pl.pallas_call(
        _out_proj_body,
        grid=(t // tm,),
        in_specs=[
            pl.BlockSpec((tm, DA_WIDTH), row),
            pl.BlockSpec((tm, GLA_WIDTH), row),
            pl.BlockSpec((tm, GLA_WIDTH), row),
            pl.BlockSpec((tm, GLA_WIDTH), lambda i: (i, PCOL_GR // GLA_WIDTH)),
            pl.BlockSpec((tm, D_MODEL), row),
            pl.BlockSpec((1, GLA_DV), const),
            pl.BlockSpec((D_MODEL, D_MODEL), const),
            pl.BlockSpec((1, D_MODEL), const),
            pl.BlockSpec((1, D_MODEL), const),
        ],
        out_specs=[pl.BlockSpec((tm, D_MODEL), row), pl.BlockSpec((tm, D_MODEL), row)],
        out_shape=[jax.ShapeDtypeStruct((t, D_MODEL), F32),
                   jax.ShapeDtypeStruct((t, D_MODEL), BF16)],
        compiler_params=pltpu.CompilerParams(
            dimension_semantics=("arbitrary",), vmem_limit_bytes=VMEM_LIMIT),
        name="out_proj",
    )(da, o_f, o_b, proj, x2d, gng, w_out, pmg, pfg)


HALO = 16


def _ffn_body(h_ref, hp_ref, hn_ref, wg_ref, wu_ref, cw_ref, cb_ref, wd_ref, x1_ref, p_ref,
              pfg_ref, wpg_ref, bpg_ref, wpp_ref, png_ref, o_ref, hext_ref, act_ref,
              *, tm, tf, seq, nff):
    i = pl.program_id(0)
    j = pl.program_id(1)

    @pl.when(j == 0)
    def _():
        first = (i * tm) % seq == 0
        last = ((i + 1) * tm) % seq == 0
        hp = hp_ref[...]
        hn = hn_ref[...]
        hext_ref[0:HALO, :] = jnp.where(first, jnp.zeros_like(hp), hp)
        hext_ref[HALO:HALO + tm, :] = h_ref[...]
        hext_ref[HALO + tm:, :] = jnp.where(last, jnp.zeros_like(hn), hn)

    gate = _dot(hext_ref[...], wg_ref[...])
    ext = tm + 2 * HALO
    cw = cw_ref[...]
    conv = (pltpu.roll(gate, 1, 0)[HALO:HALO + tm] * cw[0:1]
            + gate[HALO:HALO + tm] * cw[1:2]
            + pltpu.roll(gate, ext - 1, 0)[HALO:HALO + tm] * cw[2:3]
            + cb_ref[...])
    up = _dot(h_ref[...], wu_ref[...])
    act_ref[:, pl.ds(pl.multiple_of(j * tf, tf), tf)] = (jax.nn.gelu(conv) * up).astype(BF16)

    @pl.when(j == nff - 1)
    def _():
        x2 = x1_ref[...] + _rms(_dot(act_ref[...], wd_ref[...]), pfg_ref[...])
        e = _rms(_dot(p_ref[...].astype(BF16), wpp_ref[...]), png_ref[...])
        gate_e = jax.nn.sigmoid(_dot(x2.astype(BF16), wpg_ref[...]) + bpg_ref[...])
        o_ref[...] = x2 + gate_e * e


def _ffn(h2, x1, p2d, w_up, conv_w, conv_b, w_down, pfg, w_pg, b_pg, w_pp, png, seq, tm, tf):
    t = h2.shape[0]
    nff = D_FF // tf
    nhalo = t // HALO
    per = tm // HALO
    row = lambda i, j: (i, 0)
    const = lambda i, j: (0, 0)
    return pl.pallas_call(
        functools.partial(_ffn_body, tm=tm, tf=tf, seq=seq, nff=nff),
        grid=(t // tm, nff),
        in_specs=[
            pl.BlockSpec((tm, D_MODEL), row),
            pl.BlockSpec((HALO, D_MODEL), lambda i, j: (jnp.maximum(i * per - 1, 0), 0)),
            pl.BlockSpec((HALO, D_MODEL), lambda i, j: (jnp.minimum((i + 1) * per, nhalo - 1), 0)),
            pl.BlockSpec((D_MODEL, tf), lambda i, j: (0, j)),
            pl.BlockSpec((D_MODEL, tf), lambda i, j: (0, nff + j)),
            pl.BlockSpec((3, tf), lambda i, j: (0, j)),
            pl.BlockSpec((1, tf), lambda i, j: (0, j)),
            pl.BlockSpec((D_FF, D_MODEL), const, pipeline_mode=pl.Buffered(1)),
            pl.BlockSpec((tm, D_MODEL), row),
            pl.BlockSpec((tm, PLE_DIM), row),
            pl.BlockSpec((1, D_MODEL), const),
            pl.BlockSpec((D_MODEL, D_MODEL), const, pipeline_mode=pl.Buffered(1)),
            pl.BlockSpec((1, D_MODEL), const),
            pl.BlockSpec((PLE_DIM, D_MODEL), const, pipeline_mode=pl.Buffered(1)),
            pl.BlockSpec((1, D_MODEL), const),
        ],
        out_specs=pl.BlockSpec((tm, D_MODEL), row),
        out_shape=jax.ShapeDtypeStruct((t, D_MODEL), F32),
        scratch_shapes=[
            pltpu.VMEM((tm + 2 * HALO, D_MODEL), BF16),
            pltpu.VMEM((tm, D_FF), BF16),
        ],
        compiler_params=pltpu.CompilerParams(
            dimension_semantics=("arbitrary", "arbitrary"), vmem_limit_bytes=VMEM_LIMIT),
        name="conv_ffn",
    )(h2, h2, h2, w_up, w_up, conv_w, conv_b, w_down, x1, p2d, pfg, w_pg, b_pg, w_pp, png)


def _prep_weights(l, w_in, gla_w_gate_f, gla_b_gate_f, gla_w_gate_b, gla_b_gate_b, da_lq1,
                  da_lk1, da_lq2, da_lk2):
    w = w_in[l]
    qscale = jnp.concatenate([
        jnp.full((DA_WIDTH,), DA_QK_DIM ** -0.5 * LOG2E, F32), jnp.ones((2 * DA_WIDTH,), F32),
        jnp.full((GLA_QK_WIDTH,), GLA_DK ** -0.5, F32),
        jnp.ones((GLA_QK_WIDTH + 2 * GLA_WIDTH,), F32)])
    w_main = (w[:, :MAIN_WIDTH] * qscale).astype(BF16)
    w_lr = jnp.zeros((D_MODEL, LANES), F32).at[:, :2 * GLA_GATE_RANK].set(w[:, MAIN_WIDTH:])
    wg = jnp.zeros((LANES, 2 * GLA_QK_WIDTH), F32)
    wg = wg.at[:GLA_GATE_RANK, :GLA_QK_WIDTH].set(gla_w_gate_f[l])
    wg = wg.at[GLA_GATE_RANK:2 * GLA_GATE_RANK, GLA_QK_WIDTH:].set(gla_w_gate_b[l])
    bg = jnp.concatenate([gla_b_gate_f[l], gla_b_gate_b[l]])[None, :]
    lamvec = jnp.zeros((8, LANES), F32)
    for r, vec in enumerate((da_lq1, da_lk1, da_lq2, da_lk2)):
        lamvec = lamvec.at[r, :DA_QK_DIM].set(vec[l].astype(F32))
    return w_main, w_lr.astype(BF16), wg.astype(BF16), bg, lamvec


ROW_TILE = 512
ATTN_TILE = 1024
GLA_ROWS = 512
FF_TILE = 2048


def _attn_tile(seq):
    tile = min(seq, ATTN_TILE)
    while tile > LANES:
        resident = 2 * 2 * seq * (LANES + DA_V_DIM + ONES_ROWS)
        tiles = tile * tile * (2 * NSLOT * (4 + 2) + 2 * 4)
        if resident + tiles <= VMEM_LIMIT * 3 // 4:
            break
        tile //= 2
    return tile


def _layer(x2d, p2d, batch, seq, lam_init, wts):
    tm = min(seq, ROW_TILE)
    proj, q_t, v_t, lr, kn = _in_proj(x2d, wts["pre_mix_g"], wts["w_main"], wts["w_lr"], batch, seq, tm)
    da = _diff_attention(proj, q_t, v_t, kn, wts["lamvec"], wts["da_norm_g"], batch, seq,
                         _attn_tile(seq), lam_init)
    o_f, o_b = _gla(proj, lr, wts["wg"], wts["bg"], batch, seq, min(seq, GLA_ROWS))
    x1, h2 = _out_proj(da, o_f, o_b, proj, x2d, wts["gla_norm_g"], wts["w_out"],
                       wts["post_mix_g"], wts["pre_ffn_g"], tm)
    return _ffn(h2, x1, p2d, wts["w_ffn_up"], wts["ffn_conv_w"], wts["ffn_conv_b"],
                wts["w_ffn_down"], wts["post_ffn_g"], wts["w_ple_gate"], wts["b_ple_gate"],
                wts["w_ple_proj"], wts["ple_norm_g"], seq, tm, FF_TILE)


def kernel(x_prompt, x_sample, p_prompt, p_sample, pre_mix_g, w_in, da_lq1, da_lk1, da_lq2, da_lk2, da_norm_g, gla_w_gate_f, gla_b_gate_f, gla_w_gate_b, gla_b_gate_b, gla_norm_g, w_out, post_mix_g, pre_ffn_g, w_ffn_up, ffn_conv_w, ffn_conv_b, w_ffn_down, post_ffn_g, w_ple_gate, b_ple_gate, w_ple_proj, ple_norm_g):
    depth = w_in.shape[0]
    layers = []
    for l in range(depth):
        w_main, w_lr, wg, bg, lamvec = _prep_weights(
            l, w_in, gla_w_gate_f, gla_b_gate_f, gla_w_gate_b, gla_b_gate_b,
            da_lq1, da_lk1, da_lq2, da_lk2)
        row = lambda a: a[l][None, :].astype(F32)
        layers.append(dict(
            w_main=w_main, w_lr=w_lr, wg=wg, bg=bg, lamvec=lamvec,
            pre_mix_g=row(pre_mix_g), da_norm_g=row(da_norm_g), gla_norm_g=row(gla_norm_g),
            w_out=w_out[l].astype(BF16), post_mix_g=row(post_mix_g), pre_ffn_g=row(pre_ffn_g),
            w_ffn_up=w_ffn_up[l].astype(BF16), ffn_conv_w=ffn_conv_w[l].astype(F32),
            ffn_conv_b=row(ffn_conv_b), w_ffn_down=w_ffn_down[l].astype(BF16),
            post_ffn_g=row(post_ffn_g), w_ple_gate=w_ple_gate[l].astype(BF16),
            b_ple_gate=row(b_ple_gate), w_ple_proj=w_ple_proj[l].astype(BF16),
            ple_norm_g=row(ple_norm_g)))

    def trunk(x, p):
        batch, seq, _ = x.shape
        t = batch * seq
        x2d = x.reshape(t, D_MODEL)
        for l in range(depth):
            lam_init = 0.8 - 0.6 * math.exp(-0.3 * l)
            x2d = _layer(x2d, p[l].reshape(t, PLE_DIM), batch, seq, lam_init, layers[l])
        return x2d.reshape(batch, seq, D_MODEL)

    return (trunk(x_prompt, p_prompt), trunk(x_sample, p_sample))
```

```python
import functools
import math

import jax
import jax.numpy as jnp
from jax import lax
from jax.experimental import pallas as pl
from jax.experimental.pallas import tpu as pltpu

F32 = jnp.float32
BF16 = jnp.bfloat16

D_MODEL = 1024
PLE_DIM = 256
DA_HEADS = 4
DA_QK_DIM = 64
DA_V_DIM = 128
DA_WIDTH = DA_HEADS * DA_V_DIM
GLA_HEADS = 4
GLA_DK = 64
GLA_DV = 128
GLA_QK_WIDTH = GLA_HEADS * GLA_DK
GLA_WIDTH = GLA_HEADS * GLA_DV
GLA_GATE_RANK = 16
GLA_GATE_NORM = 16.0
GLA_CHUNK = 64
D_FF = 4 * D_MODEL
NORM_EPS = 1e-6
LANES = 128
MAIN_WIDTH = 3 * DA_WIDTH + 2 * GLA_QK_WIDTH + 2 * GLA_WIDTH
NEG_BIG = -1e30
LOG2E = math.log2(math.e)
ZERO_PROB_GAP = 138.0
ONES_ROWS = 16
NSLOT = 2
V7X_VMEM_BYTES = 64 * 1024 * 1024
VMEM_LIMIT = V7X_VMEM_BYTES * 7 // 8

COL_DA_Q = 0
COL_DA_K = DA_WIDTH
COL_DA_V = 2 * DA_WIDTH
COL_GLA = 3 * DA_WIDTH
PROJ_WIDTH = DA_WIDTH + 2 * GLA_QK_WIDTH + 2 * GLA_WIDTH
PCOL_DA_K = 0
PCOL_GQ = DA_WIDTH
PCOL_GK = PCOL_GQ + GLA_QK_WIDTH
PCOL_GV = PCOL_GK + GLA_QK_WIDTH
PCOL_GR = PCOL_GV + GLA_WIDTH


def _rms(x, g):
    return x * lax.rsqrt(jnp.mean(x * x, axis=-1, keepdims=True) + NORM_EPS) * g


def _nt_dot(a, b):
    return lax.dot_general(a, b, (((1,), (1,)), ((), ())), preferred_element_type=F32)


def _tn_dot(a, b):
    return lax.dot_general(a, b, (((0,), (0,)), ((), ())), preferred_element_type=F32)


def _dot(a, b):
    return jnp.dot(a, b, preferred_element_type=F32)


def _in_proj_body(x_ref, g_ref, wm_ref, wlr_ref, gsel_ref, proj_ref, qt_ref, vt_ref, lr_ref, kn_ref):
    h = _rms(x_ref[...], g_ref[...]).astype(BF16)
    main = _dot(h, wm_ref[...])
    proj_ref[:, :DA_WIDTH] = main[:, COL_DA_K:COL_DA_K + DA_WIDTH].astype(BF16)
    proj_ref[:, DA_WIDTH:] = main[:, COL_GLA:].astype(BF16)
    qt_ref[...] = main[:, COL_DA_Q:COL_DA_Q + DA_WIDTH].T.astype(BF16)
    for hd in range(DA_HEADS):
        c0 = COL_DA_V + hd * DA_V_DIM
        vt_ref[hd, :DA_V_DIM, :] = main[:, c0:c0 + DA_V_DIM].T.astype(BF16)
        vt_ref[hd, DA_V_DIM:, :] = jnp.ones((ONES_ROWS, main.shape[0]), BF16)
    lr_ref[...] = _dot(h, wlr_ref[...]).astype(BF16)
    k = main[:, COL_DA_K:COL_DA_K + DA_WIDTH]
    kn_ref[...] = jnp.max(_dot((k * k).astype(BF16), gsel_ref[...]), axis=0, keepdims=True)


def _in_proj(x2d, g, w_main, w_lr, batch, seq, tm):
    t = x2d.shape[0]
    per_seq = seq // tm
    va_rows = DA_V_DIM + ONES_ROWS
    col_group = jnp.arange(DA_WIDTH, dtype=jnp.int32)[:, None] // DA_QK_DIM
    gsel = (col_group == jnp.arange(LANES, dtype=jnp.int32)[None, :]).astype(BF16)
    return pl.pallas_call(
        _in_proj_body,
        grid=(t // tm,),
        in_specs=[
            pl.BlockSpec((tm, D_MODEL), lambda i: (i, 0)),
            pl.BlockSpec((1, D_MODEL), lambda i: (0, 0)),
            pl.BlockSpec((D_MODEL, MAIN_WIDTH), lambda i: (0, 0)),
            pl.BlockSpec((D_MODEL, LANES), lambda i: (0, 0)),
            pl.BlockSpec((DA_WIDTH, LANES), lambda i: (0, 0)),
        ],
        out_specs=[
            pl.BlockSpec((tm, PROJ_WIDTH), lambda i: (i, 0)),
            pl.BlockSpec((None, DA_WIDTH, tm), lambda i: (i // per_seq, 0, i % per_seq)),
            pl.BlockSpec((None, DA_HEADS, va_rows, tm),
                         lambda i: (i // per_seq, 0, 0, i % per_seq)),
            pl.BlockSpec((tm, LANES), lambda i: (i, 0)),
            pl.BlockSpec((None, 1, LANES), lambda i: (i, 0, 0)),
        ],
        out_shape=[
            jax.ShapeDtypeStruct((t, PROJ_WIDTH), BF16),
            jax.ShapeDtypeStruct((batch, DA_WIDTH, seq), BF16),
            jax.ShapeDtypeStruct((batch, DA_HEADS, va_rows, seq), BF16),
            jax.ShapeDtypeStruct((t, LANES), BF16),
            jax.ShapeDtypeStruct((t // tm, 1, LANES), F32),
        ],
        compiler_params=pltpu.CompilerParams(
            dimension_semantics=("arbitrary",), vmem_limit_bytes=VMEM_LIMIT),
        name="in_proj",
    )(x2d, g, w_main, w_lr, gsel)


def _split3(x):
    hi = x.astype(BF16).astype(F32)
    mid = (x - hi).astype(BF16).astype(F32)
    lo = (x - hi - mid).astype(BF16).astype(F32)
    return hi, mid, lo


def _attn_consts(tq):
    i = jnp.arange(tq, dtype=jnp.int32)
    lo = (i & 255).astype(F32)
    hi = (i - (i & 255)).astype(F32)
    slopes = jnp.asarray([2.0 ** (-8.0 * (h + 1) / DA_HEADS) for h in range(DA_HEADS)], F32) * LOG2E
    pieces = jnp.stack([p for piece in _split3(slopes) for p in (piece, piece)], axis=1)
    half = jnp.zeros((DA_HEADS, DA_QK_DIM, tq), F32).at[:, :6].set(
        jnp.broadcast_to(pieces[:, :, None], (DA_HEADS, 6, tq)))
    zeros = jnp.zeros_like(half)
    left = jnp.stack([jnp.concatenate([zeros, half], 1),
                      jnp.concatenate([half, zeros], 1)], axis=1)
    augq = jnp.stack([left, -left], axis=1).astype(BF16)
    kcols = jnp.stack([lo, hi, lo, hi, lo, hi], axis=-1)
    khalf = jnp.zeros((tq, DA_QK_DIM), F32).at[:, :6].set(kcols)
    kz = jnp.zeros_like(khalf)
    augk = jnp.stack([jnp.concatenate([kz, khalf], -1),
                      jnp.concatenate([khalf, kz], -1)], axis=0).astype(BF16)
    qb = slopes[:, None] * i.astype(F32)[None, :]
    qbias = jnp.stack([-qb, qb], axis=1)[:, :, None, :]
    d = (i[None, :] - i[:, None]).astype(F32)
    corr = 2.0 * slopes[:, None, None] * jnp.minimum(d, 0.0)[None]
    return augq, augk, qbias, corr


def _attn_body(lam_ref, qt_ref, k_ref, vt_ref, kn_ref, augq_ref, augk_ref, qbias_ref, corr_ref, gn_ref,
               o_ref, qa_ref, *scratch, tq, nk, nslot, lam_init):
    grab = lambda g: [scratch[2 * (g * nslot + sl):2 * (g * nslot + sl) + 2] for sl in range(nslot)]
    s_ref, p_ref, cmax_ref, alpha_ref = grab(0), grab(1), grab(2), grab(3)
    acc_ref, m_ref = scratch[8 * nslot:8 * nslot + 2], scratch[8 * nslot + 2:8 * nslot + 4]
    h = pl.program_id(1)
    qi = pl.program_id(2)
    slope = jnp.where(h == 0, 2.0 ** -2, jnp.where(h == 1, 2.0 ** -4,
                      jnp.where(h == 2, 2.0 ** -6, 2.0 ** -8))).astype(F32) * LOG2E

    qt = qt_ref[...]
    row = lax.broadcasted_iota(jnp.int32, (LANES, tq), 0)
    for side in range(2):
        qa_ref[2 * side] = jnp.where(row < DA_QK_DIM, qt, augq_ref[side, 0])
        qa_ref[2 * side + 1] = jnp.where(row >= DA_QK_DIM, qt, augq_ref[side, 1])
    for mp in range(2):
        m_ref[mp][...] = jnp.full(m_ref[mp].shape, NEG_BIG, F32)
        acc_ref[mp][...] = jnp.zeros(acc_ref[mp].shape, F32)
    lane = lax.broadcasted_iota(jnp.int32, (tq, LANES), 1)

    def scores(kt, side, slot, diag):
        k = k_ref[pl.ds(pl.multiple_of(kt * tq, tq), tq), :]
        ka = (jnp.where(lane < DA_QK_DIM, k, augk_ref[0]),
              jnp.where(lane >= DA_QK_DIM, k, augk_ref[1]))
        for mp in range(2):
            s = _dot(ka[mp], qa_ref[2 * side + mp])
            if diag:
                s = s + corr_ref[...]
            s_ref[slot][mp][...] = s
            cmax_ref[slot][mp][...] = jnp.max(s, axis=0, keepdims=True) + qbias_ref[side]

    scores(qi, 0, 0, True)

    qf = qt.astype(F32)
    lane1 = lax.broadcasted_iota(jnp.int32, (1, LANES), 1)
    kn2 = jnp.max(kn_ref[...], axis=0)
    gap = None
    for mp in range(2):
        rows = slice(mp * DA_QK_DIM, (mp + 1) * DA_QK_DIM)
        qn2 = jnp.sum(qf[rows] * qf[rows], axis=0, keepdims=True)
        kn2_mp = jnp.max(jnp.where(lane1 == 2 * h + mp, kn2, 0.0), axis=1, keepdims=True)
        g = jnp.max(1.02 * jnp.sqrt(qn2 * kn2_mp) - cmax_ref[0][mp][...], axis=1, keepdims=True)
        gap = g if gap is None else jnp.maximum(gap, g)
    far = ((gap + ZERO_PROB_GAP) / slope - 1.0) / tq
    reach = jnp.where(far < nk, jnp.maximum(jnp.ceil(far), 0.0), float(nk)).astype(jnp.int32)[0, 0]
    n_left = jnp.minimum(qi, reach)
    n_tiles = n_left + jnp.minimum(nk - 1 - qi, reach)

    def key_tile(t):
        return jnp.where(t <= n_left, qi - t, qi + (t - n_left))

    def tile_side(t):
        return (t > n_left).astype(jnp.int32)

    def tile_const(t):
        return -slope * (jnp.abs(qi - key_tile(t)) * tq).astype(F32)

    def probs(t, slot):
        shift = tile_const(t) + qbias_ref[tile_side(t)]
        for mp in range(2):
            m_old = m_ref[mp][...]
            m_new = jnp.maximum(m_old, cmax_ref[slot][mp][...] + tile_const(t))
            alpha_ref[slot][mp][...] = jnp.exp2(m_old - m_new)
            p_ref[slot][mp][...] = jnp.exp2(s_ref[slot][mp][...] - (m_new - shift)).astype(BF16)
            m_ref[mp][...] = m_new

    def accumulate(t, slot):
        va = vt_ref[:, pl.ds(pl.multiple_of(key_tile(t) * tq, tq), tq)]
        for mp in range(2):
            acc_ref[mp][...] = (alpha_ref[slot][mp][...] * acc_ref[mp][...]
                                + _dot(va, p_ref[slot][mp][...]))

    def step(t, slot):
        prev = (slot - 1) % nslot

        @pl.when(t <= n_tiles)
        def _():
            probs(t - 1, prev)
            scores(key_tile(t), tile_side(t), slot, False)
            accumulate(t - 1, prev)

    def trip(i, carry):
        for r in range(1, nslot + 1):
            step(nslot * i + r, r % nslot)
        return carry

    lax.fori_loop(0, (n_tiles + nslot - 1) // nslot, trip, 0)
    for r in range(nslot):
        @pl.when(n_tiles % nslot == r)
        def _(r=r):
            probs(n_tiles, r)
            accumulate(n_tiles, r)

    lv = lam_ref[...]
    lam = (jnp.exp(jnp.sum(lv[0:1] * lv[1:2], axis=-1, keepdims=True))
           - jnp.exp(jnp.sum(lv[2:3] * lv[3:4], axis=-1, keepdims=True)) + lam_init)
    a0 = acc_ref[0][...]
    a1 = acc_ref[1][...]
    o0 = a0[:DA_V_DIM] * (1.0 / a0[DA_V_DIM:DA_V_DIM + 1])
    o1 = a1[:DA_V_DIM] * (1.0 / a1[DA_V_DIM:DA_V_DIM + 1])
    out = o0 - lam * o1
    inv = lax.rsqrt(jnp.mean(out * out, axis=0, keepdims=True) + NORM_EPS)
    out = out * inv * gn_ref[...] * (1.0 - lam_init)
    o_ref[...] = out.T.astype(o_ref.dtype)


def _diff_attention(proj, q_t, v_t, kn, lamvec, gn, batch, seq, tq, lam_init):
    t = proj.shape[0]
    nk = seq // tq
    augq, augk, qbias, corr = _attn_consts(tq)
    gnb = jnp.broadcast_to(gn.reshape(DA_V_DIM, 1), (DA_V_DIM, tq))
    nslot = NSLOT
    body = functools.partial(_attn_body, tq=tq, nk=nk, nslot=nslot, lam_init=lam_init)
    kblk = PCOL_DA_K // LANES
    return pl.pallas_call(
        body,
        grid=(batch, DA_HEADS, nk),
        in_specs=[
            pl.BlockSpec((8, LANES), lambda b, h, i: (0, 0)),
            pl.BlockSpec((None, LANES, tq), lambda b, h, i: (b, h, i)),
            pl.BlockSpec((seq, LANES), lambda b, h, i: (b, kblk + h)),
            pl.BlockSpec((None, None, DA_V_DIM + ONES_ROWS, seq), lambda b, h, i: (b, h, 0, 0)),
            pl.BlockSpec((kn.shape[0] // batch, 1, LANES), lambda b, h, i: (b, 0, 0)),
            pl.BlockSpec((None, 2, 2, LANES, tq), lambda b, h, i: (h, 0, 0, 0, 0)),
            pl.BlockSpec((2, tq, LANES), lambda b, h, i: (0, 0, 0)),
            pl.BlockSpec((None, 2, 1, tq), lambda b, h, i: (h, 0, 0, 0)),
            pl.BlockSpec((None, tq, tq), lambda b, h, i: (h, 0, 0)),
            pl.BlockSpec((DA_V_DIM, tq), lambda b, h, i: (0, 0)),
        ],
        out_specs=pl.BlockSpec((tq, LANES), lambda b, h, i: (b * nk + i, h)),
        out_shape=jax.ShapeDtypeStruct((t, DA_WIDTH), BF16),
        scratch_shapes=[
            pltpu.VMEM((4, LANES, tq), BF16),
        ] + [pltpu.VMEM((tq, tq), F32)] * (2 * nslot)
        + [pltpu.VMEM((tq, tq), BF16)] * (2 * nslot)
        + [pltpu.VMEM((1, tq), F32)] * (2 * nslot)
        + [pltpu.VMEM((1, tq), F32)] * (2 * nslot)
        + [pltpu.VMEM((DA_V_DIM + ONES_ROWS, tq), F32)] * 2
        + [pltpu.VMEM((1, tq), F32)] * 2,
        compiler_params=pltpu.CompilerParams(
            dimension_semantics=("arbitrary", "arbitrary", "arbitrary"),
            vmem_limit_bytes=VMEM_LIMIT),
        name="diff_attention",
    )(lamvec, q_t, proj, v_t, kn, augq, augk, qbias, corr, gnb)


def _log_sigmoid(z):
    return jnp.minimum(z, 0.0) - jnp.log1p(jnp.exp(-jnp.abs(z)))


def _gla_chunk(q, k, v, g, st_ref, reverse):
    c = GLA_CHUNK
    q = q.astype(F32)
    k = k.astype(F32)
    row = lax.broadcasted_iota(jnp.int32, (c, GLA_QK_WIDTH), 0)
    lane = lax.broadcasted_iota(jnp.int32, (c, GLA_QK_WIDTH), 1)
    b = g
    shift = 1
    while shift < c:
        b = b + jnp.where(row >= shift, pltpu.roll(b, shift, 0), 0.0)
        shift *= 2
    tot = b[c - 1:c, :]
    if reverse:
        b = tot - b + g
    q_t = (q * jnp.exp(b)).astype(BF16)
    k_t = (k * jnp.exp(-b)).astype(BF16)
    k_end = (k * jnp.exp(tot - b)).astype(BF16)
    decay = jnp.exp(tot)

    head = lane // GLA_DK
    zero_k = jnp.zeros_like(k_t)
    k_bd = jnp.concatenate([jnp.where(head == h, k_t, zero_k) for h in range(GLA_HEADS)], axis=0)
    att = _nt_dot(q_t, k_bd)
    pos = lane % c
    keep = (pos > row) if reverse else (pos <= row)
    att = jnp.where(keep, att, 0.0).astype(BF16)

    vhead = lax.broadcasted_iota(jnp.int32, (c, GLA_WIDTH), 1) // GLA_DV
    zero_v = jnp.zeros_like(v)
    v_bd = jnp.concatenate([jnp.where(vhead == h, v, zero_v) for h in range(GLA_HEADS)], axis=0)
    st = st_ref[...]
    st_b = st.astype(BF16)
    shead = lax.broadcasted_iota(jnp.int32, st.shape, 1) // GLA_DK
    zero_s = jnp.zeros_like(st_b)
    st_bd = jnp.concatenate([jnp.where(shead == h, st_b, zero_s) for h in range(GLA_HEADS)], axis=0)
    out = _dot(att, v_bd) + _nt_dot(q_t, st_bd)

    kv_t = _tn_dot(v, k_end)
    upd = kv_t[(GLA_HEADS - 1) * GLA_DV:]
    for h in range(GLA_HEADS - 2, -1, -1):
        upd = jnp.where(shead == h, kv_t[h * GLA_DV:(h + 1) * GLA_DV], upd)
    st_ref[...] = decay * st + upd
    return out


def _gla_body(qf_ref, kf_ref, vf_ref, lrf_ref, qb_ref, kb_ref, vb_ref, lrb_ref, wg_ref, bg_ref,
              of_ref, ob_ref, stf_ref, stb_ref, *, rows):
    @pl.when(pl.program_id(1) == 0)
    def _():
        stf_ref[...] = jnp.zeros(stf_ref.shape, F32)
        stb_ref[...] = jnp.zeros(stb_ref.shape, F32)

    wg = wg_ref[...]
    bg = bg_ref[...]
    zf = _dot(lrf_ref[...], wg[:, :GLA_QK_WIDTH]) + bg[:, :GLA_QK_WIDTH]
    zb = _dot(lrb_ref[...], wg[:, GLA_QK_WIDTH:]) + bg[:, GLA_QK_WIDTH:]
    gf = _log_sigmoid(zf) / GLA_GATE_NORM
    gb = _log_sigmoid(zb) / GLA_GATE_NORM
    nc = rows // GLA_CHUNK
    for ci in range(nc):
        sf = slice(ci * GLA_CHUNK, (ci + 1) * GLA_CHUNK)
        of_ref[sf, :] = _gla_chunk(qf_ref[sf, :], kf_ref[sf, :], vf_ref[sf, :], gf[sf, :],
                                   stf_ref, False)
        cj = nc - 1 - ci
        sb = slice(cj * GLA_CHUNK, (cj + 1) * GLA_CHUNK)
        ob_ref[sb, :] = _gla_chunk(qb_ref[sb, :], kb_ref[sb, :], vb_ref[sb, :], gb[sb, :],
                                   stb_ref, True)


def _gla(proj, lr, wg, bg, batch, seq, rows):
    t = proj.shape[0]
    nb = seq // rows
    qblk = PCOL_GQ // GLA_QK_WIDTH
    kblk = PCOL_GK // GLA_QK_WIDTH
    vblk = PCOL_GV // GLA_WIDTH

    def fwd(col):
        return lambda b, i: (b * nb + i, col)

    def bwd(col):
        return lambda b, i: (b * nb + nb - 1 - i, col)

    def specs(ix):
        return [
            pl.BlockSpec((rows, GLA_QK_WIDTH), ix(qblk)),
            pl.BlockSpec((rows, GLA_QK_WIDTH), ix(kblk)),
            pl.BlockSpec((rows, GLA_WIDTH), ix(vblk)),
            pl.BlockSpec((rows, LANES), ix(0)),
        ]

    return pl.pallas_call(
        functools.partial(_gla_body, rows=rows),
        grid=(batch, nb),
        in_specs=specs(fwd) + specs(bwd) + [
            pl.BlockSpec((LANES, 2 * GLA_QK_WIDTH), lambda b, i: (0, 0)),
            pl.BlockSpec((1, 2 * GLA_QK_WIDTH), lambda b, i: (0, 0)),
        ],
        out_specs=[
            pl.BlockSpec((rows, GLA_WIDTH), fwd(0)),
            pl.BlockSpec((rows, GLA_WIDTH), bwd(0)),
        ],
        out_shape=[jax.ShapeDtypeStruct((t, GLA_WIDTH), F32)] * 2,
        scratch_shapes=[pltpu.VMEM((GLA_DV, GLA_QK_WIDTH), F32)] * 2,
        compiler_params=pltpu.CompilerParams(
            dimension_semantics=("arbitrary", "arbitrary"), vmem_limit_bytes=VMEM_LIMIT),
        name="gla",
    )(proj, proj, proj, lr, proj, proj, proj, lr, wg, bg)


def _out_proj_body(da_ref, of_ref, ob_ref, gr_ref, x_ref, gng_ref, wout_ref, pmg_ref, pfg_ref,
                   x1_ref, h2_ref):
    o = of_ref[...] + ob_ref[...]
    gr = gr_ref[...].astype(F32)
    parts = []
    for h in range(GLA_HEADS):
        sl = slice(h * GLA_DV, (h + 1) * GLA_DV)
        gh = gr[:, sl]
        parts.append(_rms(o[:, sl], gng_ref[...]) * (gh * jax.nn.sigmoid(gh)))
    gla = jnp.concatenate(parts, axis=1).astype(BF16)
    mix = _dot(jnp.concatenate([da_ref[...], gla], axis=1), wout_ref[...])
    x1 = x_ref[...] + _rms(mix, pmg_ref[...])
    x1_ref[...] = x1
    h2_ref[...] = _rms(x1, pfg_ref[...]).astype(BF16)


def _out_proj(da, o_f, o_b, proj, x2d, gng, w_out, pmg, pfg, tm):
    t = x2d.shape[0]
    row = lambda i: (i, 0)
    const = lambda i: (0, 0)
    return pl.pallas_call(
        _out_proj_body,
        grid=(t // tm,),
        in_specs=[
            pl.BlockSpec((tm, DA_WIDTH), row),
            pl.BlockSpec((tm, GLA_WIDTH), row),
            pl.BlockSpec((tm, GLA_WIDTH), row),
            pl.BlockSpec((tm, GLA_WIDTH), lambda i: (i, PCOL_GR // GLA_WIDTH)),
            pl.BlockSpec((tm, D_MODEL), row),
            pl.BlockSpec((1, GLA_DV), const),
            pl.BlockSpec((D_MODEL, D_MODEL), const),
            pl.BlockSpec((1, D_MODEL), const),
            pl.BlockSpec((1, D_MODEL), const),
        ],
        out_specs=[pl.BlockSpec((tm, D_MODEL), row), pl.BlockSpec((tm, D_MODEL), row)],
        out_shape=[jax.ShapeDtypeStruct((t, D_MODEL), F32),
                   jax.ShapeDtypeStruct((t, D_MODEL), BF16)],
        compiler_params=pltpu.CompilerParams(
            dimension_semantics=("arbitrary",), vmem_limit_bytes=VMEM_LIMIT),
        name="out_proj",
    )(da, o_f, o_b, proj, x2d, gng, w_out, pmg, pfg)


HALO = 16


def _ffn_body(h_ref, hp_ref, hn_ref, wg_ref, wu_ref, cw_ref, cb_ref, wd_ref, x1_ref, p_ref,
              pfg_ref, wpg_ref, bpg_ref, wpp_ref, png_ref, o_ref, hext_ref, act_ref,
              *, tm, tf, seq, nff):
    i = pl.program_id(0)
    j = pl.program_id(1)

    @pl.when(j == 0)
    def _():
        first = (i * tm) % seq == 0
        last = ((i + 1) * tm) % seq == 0
        hp = hp_ref[...]
        hn = hn_ref[...]
        hext_ref[0:HALO, :] = jnp.where(first, jnp.zeros_like(hp), hp)
        hext_ref[HALO:HALO + tm, :] = h_ref[...]
        hext_ref[HALO + tm:, :] = jnp.where(last, jnp.zeros_like(hn), hn)

    gate = _dot(hext_ref[...], wg_ref[...])
    ext = tm + 2 * HALO
    cw = cw_ref[...]
    conv = (pltpu.roll(gate, 1, 0)[HALO:HALO + tm] * cw[0:1]
            + gate[HALO:HALO + tm] * cw[1:2]
            + pltpu.roll(gate, ext - 1, 0)[HALO:HALO + tm] * cw[2:3]
            + cb_ref[...])
    up = _dot(h_ref[...], wu_ref[...])
    act_ref[:, pl.ds(pl.multiple_of(j * tf, tf), tf)] = (jax.nn.gelu(conv) * up).astype(BF16)

    @pl.when(j == nff - 1)
    def _():
        x2 = x1_ref[...] + _rms(_dot(act_ref[...], wd_ref[...]), pfg_ref[...])
        e = _rms(_dot(p_ref[...].astype(BF16), wpp_ref[...]), png_ref[...])
        gate_e = jax.nn.sigmoid(_dot(x2.astype(BF16), wpg_ref[...]) + bpg_ref[...])
        o_ref[...] = x2 + gate_e * e


def _ffn(h2, x1, p2d, w_up, conv_w, conv_b, w_down, pfg, w_pg, b_pg, w_pp, png, seq, tm, tf):
    t = h2.shape[0]
    nff = D_FF // tf
    nhalo = t // HALO
    per = tm // HALO
    row = lambda i, j: (i, 0)
    const = lambda i, j: (0, 0)
    return pl.pallas_call(
        functools.partial(_ffn_body, tm=tm, tf=tf, seq=seq, nff=nff),
        grid=(t // tm, nff),
        in_specs=[
            pl.BlockSpec((tm, D_MODEL), row),
            pl.BlockSpec((HALO, D_MODEL), lambda i, j: (jnp.maximum(i * per - 1, 0), 0)),
            pl.BlockSpec((HALO, D_MODEL), lambda i, j: (jnp.minimum((i + 1) * per, nhalo - 1), 0)),
            pl.BlockSpec((D_MODEL, tf), lambda i, j: (0, j)),
            pl.BlockSpec((D_MODEL, tf), lambda i, j: (0, nff + j)),
            pl.BlockSpec((3, tf), lambda i, j: (0, j)),
            pl.BlockSpec((1, tf), lambda i, j: (0, j)),
            pl.BlockSpec((D_FF, D_MODEL), const, pipeline_mode=pl.Buffered(1)),
            pl.BlockSpec((tm, D_MODEL), row),
            pl.BlockSpec((tm, PLE_DIM), row),
            pl.BlockSpec((1, D_MODEL), const),
            pl.BlockSpec((D_MODEL, D_MODEL), const, pipeline_mode=pl.Buffered(1)),
            pl.BlockSpec((1, D_MODEL), const),
            pl.BlockSpec((PLE_DIM, D_MODEL), const, pipeline_mode=pl.Buffered(1)),
            pl.BlockSpec((1, D_MODEL), const),
        ],
        out_specs=pl.BlockSpec((tm, D_MODEL), row),
        out_shape=jax.ShapeDtypeStruct((t, D_MODEL), F32),
        scratch_shapes=[
            pltpu.VMEM((tm + 2 * HALO, D_MODEL), BF16),
            pltpu.VMEM((tm, D_FF), BF16),
        ],
        compiler_params=pltpu.CompilerParams(
            dimension_semantics=("arbitrary", "arbitrary"), vmem_limit_bytes=VMEM_LIMIT),
        name="conv_ffn",
    )(h2, h2, h2, w_up, w_up, conv_w, conv_b, w_down, x1, p2d, pfg, w_pg, b_pg, w_pp, png)


def _prep_weights(l, w_in, gla_w_gate_f, gla_b_gate_f, gla_w_gate_b, gla_b_gate_b, da_lq1,
                  da_lk1, da_lq2, da_lk2):
    w = w_in[l]
    qscale = jnp.concatenate([
        jnp.full((DA_WIDTH,), DA_QK_DIM ** -0.5 * LOG2E, F32), jnp.ones((2 * DA_WIDTH,), F32),
        jnp.full((GLA_QK_WIDTH,), GLA_DK ** -0.5, F32),
        jnp.ones((GLA_QK_WIDTH + 2 * GLA_WIDTH,), F32)])
    w_main = (w[:, :MAIN_WIDTH] * qscale).astype(BF16)
    w_lr = jnp.zeros((D_MODEL, LANES), F32).at[:, :2 * GLA_GATE_RANK].set(w[:, MAIN_WIDTH:])
    wg = jnp.zeros((LANES, 2 * GLA_QK_WIDTH), F32)
    wg = wg.at[:GLA_GATE_RANK, :GLA_QK_WIDTH].set(gla_w_gate_f[l])
    wg = wg.at[GLA_GATE_RANK:2 * GLA_GATE_RANK, GLA_QK_WIDTH:].set(gla_w_gate_b[l])
    bg = jnp.concatenate([gla_b_gate_f[l], gla_b_gate_b[l]])[None, :]
    lamvec = jnp.zeros((8, LANES), F32)
    for r, vec in enumerate((da_lq1, da_lk1, da_lq2, da_lk2)):
        lamvec = lamvec.at[r, :DA_QK_DIM].set(vec[l].astype(F32))
    return w_main, w_lr.astype(BF16), wg.astype(BF16), bg, lamvec


ROW_TILE = 512
ATTN_TILE = 1024
GLA_ROWS = 1024
FF_TILE = 2048


def _attn_tile(seq):
    tile = min(seq, ATTN_TILE)
    while tile > LANES:
        resident = 2 * 2 * seq * (LANES + DA_V_DIM + ONES_ROWS)
        tiles = tile * tile * (2 * NSLOT * (4 + 2) + 2 * 4)
        if resident + tiles <= VMEM_LIMIT * 3 // 4:
            break
        tile //= 2
    return tile


def _layer(x2d, p2d, batch, seq, lam_init, wts):
    tm = min(seq, ROW_TILE)
    proj, q_t, v_t, lr, kn = _in_proj(x2d, wts["pre_mix_g"], wts["w_main"], wts["w_lr"], batch, seq, tm)
    da = _diff_attention(proj, q_t, v_t, kn, wts["lamvec"], wts["da_norm_g"], batch, seq,
                         _attn_tile(seq), lam_init)
    o_f, o_b = _gla(proj, lr, wts["wg"], wts["bg"], batch, seq, min(seq, GLA_ROWS))
    x1, h2 = _out_proj(da, o_f, o_b, proj, x2d, wts["gla_norm_g"], wts["w_out"],
                       wts["post_mix_g"], wts["pre_ffn_g"], tm)
    return _ffn(h2, x1, p2d, wts["w_ffn_up"], wts["ffn_conv_w"], wts["ffn_conv_b"],
                wts["w_ffn_down"], wts["post_ffn_g"], wts["w_ple_gate"], wts["b_ple_gate"],
                wts["w_ple_proj"], wts["ple_norm_g"], seq, tm, FF_TILE)


def kernel(x_prompt, x_sample, p_prompt, p_sample, pre_mix_g, w_in, da_lq1, da_lk1, da_lq2, da_lk2, da_norm_g, gla_w_gate_f, gla_b_gate_f, gla_w_gate_b, gla_b_gate_b, gla_norm_g, w_out, post_mix_g, pre_ffn_g, w_ffn_up, ffn_conv_w, ffn_conv_b, w_ffn_down, post_ffn_g, w_ple_gate, b_ple_gate, w_ple_proj, ple_norm_g):
    depth = w_in.shape[0]
    layers = []
    for l in range(depth):
        w_main, w_lr, wg, bg, lamvec = _prep_weights(
            l, w_in, gla_w_gate_f, gla_b_gate_f, gla_w_gate_b, gla_b_gate_b,
            da_lq1, da_lk1, da_lq2, da_lk2)
        row = lambda a: a[l][None, :].astype(F32)
        layers.append(dict(
            w_main=w_main, w_lr=w_lr, wg=wg, bg=bg, lamvec=lamvec,
            pre_mix_g=row(pre_mix_g), da_norm_g=row(da_norm_g), gla_norm_g=row(gla_norm_g),
            w_out=w_out[l].astype(BF16), post_mix_g=row(post_mix_g), pre_ffn_g=row(pre_ffn_g),
            w_ffn_up=w_ffn_up[l].astype(BF16), ffn_conv_w=ffn_conv_w[l].astype(F32),
            ffn_conv_b=row(ffn_conv_b), w_ffn_down=w_ffn_down[l].astype(BF16),
            post_ffn_g=row(post_ffn_g), w_ple_gate=w_ple_gate[l].astype(BF16),
            b_ple_gate=row(b_ple_gate), w_ple_proj=w_ple_proj[l].astype(BF16),
            ple_norm_g=row(ple_norm_g)))

    def trunk(x, p):
        batch, seq, _ = x.shape
        t = batch * seq
        x2d = x.reshape(t, D_MODEL)
        for l in range(depth):
            lam_init = 0.8 - 0.6 * math.exp(-0.3 * l)
            x2d = _layer(x2d, p[l].reshape(t, PLE_DIM), batch, seq, lam_init, layers[l])
        return x2d.reshape(batch, seq, D_MODEL)

    return (trunk(x_prompt, p_prompt), trunk(x_sample, p_sample))
```
